```python
import math, functools
import jax
import jax.numpy as jnp
from jax import lax
import numpy as np

D_MODEL = 1024
BATCH = 4
SEQ = 8192
DEPTH = 4

CTX_LEN = 256
GRID_W = 64
D_MIX = D_MODEL
MIXER_W = D_MIX // 4
CHUNK = 32
NORM_EPS = 1e-6
ROPE_BASE = 10000.0
LB_FLOOR = 1e-30
LB_CEIL = 1.0 - 1e-6

RET_HEADS = 4
RET_DK = MIXER_W // RET_HEADS
RET_DV = MIXER_W // RET_HEADS

S5_GROUP = 16
S5_GROUPS = MIXER_W // S5_GROUP
S5_STATE = 64

HG_HEADS = 4
HG_DK = MIXER_W // HG_HEADS
HG_DV = MIXER_W // HG_HEADS

GLA_HEADS = 4
GLA_DK = MIXER_W // (2 * GLA_HEADS)
GLA_DV = MIXER_W // GLA_HEADS
GLA_RANK = 16
GLA_TAU = 16.0

D_FF = 2816
N_EXPERTS = 8
TOP_K = 2
D_FF_EXPERT = 2816
MOE_BLOCK = 256
N_DENSE = (DEPTH + 1) // 2
N_MOE = DEPTH // 2

COL_SPLITS = (
    ('ret_q', RET_HEADS * RET_DK), ('ret_k', RET_HEADS * RET_DK),
    ('ret_v', RET_HEADS * RET_DV), ('ret_g', RET_HEADS * RET_DV),
    ('s5_u', MIXER_W),
    ('hg_q', HG_HEADS * HG_DK), ('hg_ff', HG_HEADS * HG_DK), ('hg_fb', HG_HEADS * HG_DK),
    ('hg_i', HG_HEADS * HG_DV), ('hg_g', HG_HEADS * HG_DV),
    ('gla_q', GLA_HEADS * GLA_DK), ('gla_k', GLA_HEADS * GLA_DK),
    ('gla_v', GLA_HEADS * GLA_DV), ('gla_g', GLA_HEADS * GLA_DV),
    ('gla_af', GLA_RANK), ('gla_ab', GLA_RANK),
)
D_IN = sum(w for _, w in COL_SPLITS)

kernel_name = 'hybrid_ret_s5_hgrn2_gla_moe_dit'


def rmsnorm(x, w):
    xf = x.astype(jnp.float32)
    y = xf * lax.rsqrt(jnp.mean(xf * xf, axis=-1, keepdims=True) + NORM_EPS)
    return (y * w.astype(jnp.float32)).astype(x.dtype)


def head_norm(o, w, center):
    of = o.astype(jnp.float32)
    if center:
        of = of - jnp.mean(of, axis=-1, keepdims=True)
    of = of * lax.rsqrt(jnp.mean(of * of, axis=-1, keepdims=True) + NORM_EPS)
    return (of.reshape(o.shape[0], o.shape[1], -1) * w.astype(jnp.float32)).astype(o.dtype)


def modulate(h, shift, scale):
    return h * (1 + scale) + shift


def split_cols(p):
    names = [n for n, _ in COL_SPLITS]
    offsets = np.cumsum([w for _, w in COL_SPLITS])[:-1].tolist()
    return dict(zip(names, jnp.split(p, offsets, axis=-1)))


def to_heads(t, n_heads):
    return t.reshape(t.shape[0], t.shape[1], n_heads, t.shape[-1] // n_heads)


def rope_1d(t, pos):
    half = t.shape[-1] // 2
    freqs = ROPE_BASE ** (-jnp.arange(half, dtype=jnp.float32) / half)
    ang = pos[:, None] * freqs[None, :]
    cos = jnp.cos(ang)[None, :, None, :]
    sin = jnp.sin(ang)[None, :, None, :]
    t1 = t[..., :half].astype(jnp.float32)
    t2 = t[..., half:].astype(jnp.float32)
    return jnp.concatenate([t1 * cos - t2 * sin, t1 * sin + t2 * cos], axis=-1).astype(t.dtype)


def axial_rope(t, row, col):
    d = t.shape[-1] // 2
    return jnp.concatenate([rope_1d(t[..., :d], row), rope_1d(t[..., d:], col)], axis=-1)


def chunk_gated_scan(inputs, h0):
    q, k, v, log_g = (a.astype(jnp.float32) for a in inputs)
    bsz, seq, n_heads, dk = q.shape
    dv = v.shape[-1]
    n_chunks = seq // CHUNK

    def chunks(a):
        return jnp.moveaxis(a.reshape(bsz, n_chunks, CHUNK, *a.shape[2:]), 1, 0)

    if h0 is None:
        h0 = jnp.zeros((bsz, n_heads, dk, dv), jnp.float32)
    causal = jnp.tril(jnp.ones((CHUNK, CHUNK), dtype=bool))[None, :, :, None, None]
    scalar_decay = log_g.shape[-1] == 1

    def step(state, blk):
        qc, kc, vc, gc = blk
        b = jnp.cumsum(gc, axis=1)
        diff = b[:, :, None] - b[:, None, :]
        decay = jnp.where(causal, jnp.exp(jnp.where(causal, diff, 0.0)), 0.0)
        if scalar_decay:
            scores = jnp.einsum('bihk,bjhk->bijh', qc, kc) * decay[..., 0]
        else:
            scores = jnp.sum(qc[:, :, None] * kc[:, None, :] * decay, axis=-1)
        out = (jnp.einsum('bijh,bjhv->bihv', scores, vc)
               + jnp.einsum('bihk,bhkv->bihv', qc * jnp.exp(b), state))
        b_last = b[:, -1]
        state = (jnp.exp(b_last)[..., None] * state
                 + jnp.einsum('bjhk,bjhv->bhkv', kc * jnp.exp(b_last[:, None] - b), vc))
        return state, out

    h_final, out = lax.scan(step, h0, tuple(chunks(a) for a in (q, k, v, log_g)))
    out = jnp.moveaxis(out, 0, 1).reshape(bsz, seq, n_heads, dv)
    return h_final, out.astype(inputs[2].dtype)


def s5_scan(prm, inputs, h0):
    lam_re, lam_im, log_dt, b_re, b_im, c_re, c_im = (a.astype(jnp.float32) for a in prm)
    (u,) = inputs
    seq = u.shape[1]
    lam = lax.complex(lam_re, lam_im)
    a_bar = jnp.exp(lam * jnp.exp(log_dt)[:, None])
    b_bar = ((a_bar - 1) / lam)[..., None] * lax.complex(b_re, b_im)
    c_mat = lax.complex(c_re, c_im)
    bu = jnp.einsum('gpc,btgc->tbgp', b_bar, u.astype(jnp.float32).astype(jnp.complex64))
    if h0 is not None:
        bu = bu.at[0].add(a_bar * h0)
    a_seq = jnp.broadcast_to(a_bar, (seq, 1) + a_bar.shape)

    def combine(e1, e2):
        return e1[0] * e2[0], e2[0] * e1[1] + e2[1]

    _, h = lax.associative_scan(combine, (a_seq, bu), axis=0)
    y = jnp.einsum('gcp,tbgp->btgc', c_mat, h).real
    return h[-1], y.astype(u.dtype)


def prefixed_bidir(run_f, run_b, ctx_f, lat_f, ctx_b, lat_b, want_ctx):
    def flip(ts):
        return tuple(t[:, ::-1] for t in ts)
    h_f, yc_f = run_f(ctx_f, None)
    _, yl_f = run_f(lat_f, h_f)
    h_b, yc_b = run_b(flip(ctx_b), None)
    _, yl_b = run_b(flip(lat_b), h_b)
    y_lat = yl_f + yl_b[:, ::-1]
    y_ctx = yc_f + yc_b[:, ::-1] if want_ctx else None
    return y_ctx, y_lat


def retention_mixer(pc, pl, gn_w, row, col, want_ctx):
    def qkv(p, rotate):
        q = to_heads(p['ret_q'], RET_HEADS)
        k = to_heads(p['ret_k'], RET_HEADS) * (RET_DK ** -0.5)
        v = to_heads(p['ret_v'], RET_HEADS)
        if rotate:
            q, k = axial_rope(q, row, col), axial_rope(k, row, col)
        return q, k, v

    log_gamma_f = jnp.log1p(-(2.0 ** (-5.0 - jnp.arange(RET_HEADS, dtype=jnp.float32))))
    log_gamma_b = log_gamma_f[::-1]

    def decay(p, lg):
        return jnp.broadcast_to(lg[None, None, :, None], (p.shape[0], p.shape[1], RET_HEADS, 1))

    qc, kc, vc = qkv(pc, False)
    ql, kl, vl = qkv(pl, True)
    yc, yl = prefixed_bidir(
        chunk_gated_scan, chunk_gated_scan,
        (qc, kc, vc, decay(pc['ret_q'], log_gamma_f)), (ql, kl, vl, decay(pl['ret_q'], log_gamma_f)),
        (qc, kc, vc, decay(pc['ret_q'], log_gamma_b)), (ql, kl, vl, decay(pl['ret_q'], log_gamma_b)),
        want_ctx)

    def post(y, p):
        return head_norm(y, gn_w, True) * jax.nn.silu(p['ret_g'])
    return (post(yc, pc) if want_ctx else None), post(yl, pl)


def s5_mixer(pc, pl, lp, want_ctx):
    names = ('s5_lam_re', 's5_lam_im', 's5_log_dt', 's5_b_re', 's5_b_im', 's5_c_re', 's5_c_im')
    run_f = functools.partial(s5_scan, tuple(lp[n][0] for n in names))
    run_b = functools.partial(s5_scan, tuple(lp[n][1] for n in names))

    def groups(u):
        return u.reshape(u.shape[0], u.shape[1], S5_GROUPS, S5_GROUP)

    uc, ul = pc['s5_u'], pl['s5_u']
    yc, yl = prefixed_bidir(run_f, run_b, (groups(uc),), (groups(ul),), (groups(uc),), (groups(ul),), want_ctx)

    def post(y, u):
        y = y.reshape(u.shape) + lp['s5_d'] * u
        z = jax.nn.gelu(y)
        return z * jax.nn.sigmoid(z @ lp['s5_glu_w'] + lp['s5_glu_b'])
    return (post(yc, uc) if want_ctx else None), post(yl, ul)


def hgrn2_mixer(pc, pl, lower_bound, norm_w, want_ctx):
    lb = lower_bound.reshape(HG_HEADS, HG_DK).astype(jnp.float32)
    log_lb = jnp.log(jnp.maximum(lb, LB_FLOOR))
    log_1m_lb = jnp.log1p(-lb)

    def inputs(p, gate_name):
        z = to_heads(p[gate_name], HG_HEADS).astype(jnp.float32)
        log_f = jnp.logaddexp(log_lb, log_1m_lb + jax.nn.log_sigmoid(z))
        k = (1.0 - lb) * jax.nn.sigmoid(-z)
        return (to_heads(p['hg_q'], HG_HEADS), k, to_heads(p['hg_i'], HG_HEADS), log_f)

    yc, yl = prefixed_bidir(chunk_gated_scan, chunk_gated_scan,
                            inputs(pc, 'hg_ff'), inputs(pl, 'hg_ff'),
                            inputs(pc, 'hg_fb'), inputs(pl, 'hg_fb'), want_ctx)

    def post(y, p):
        return head_norm(y, norm_w, False) * jax.nn.silu(p['hg_g'])
    return (post(yc, pc) if want_ctx else None), post(yl, pl)


def gla_mixer(pc, pl, wa2, ba, norm_w, want_ctx):
    def inputs(p, d):
        low = p[('gla_af', 'gla_ab')[d]]
        log_a = jax.nn.log_sigmoid((low @ wa2[d] + ba[d]).astype(jnp.float32)) / GLA_TAU
        return (to_heads(p['gla_q'], GLA_HEADS),
                to_heads(p['gla_k'], GLA_HEADS) * (GLA_DK ** -0.5),
                to_heads(p['gla_v'], GLA_HEADS),
                to_heads(log_a, GLA_HEADS))

    yc, yl = prefixed_bidir(chunk_gated_scan, chunk_gated_scan,
                            inputs(pc, 0), inputs(pl, 0), inputs(pc, 1), inputs(pl, 1), want_ctx)

    def post(y, p):
        return head_norm(y, norm_w, False) * jax.nn.silu(p['gla_g'])
    return (post(yc, pc) if want_ctx else None), post(yl, pl)


def token_mix(pc, pl, lp, lower_bound, row, col, want_ctx):
    outs = [
        retention_mixer(pc, pl, lp['ret_gn_w'], row, col, want_ctx),
        s5_mixer(pc, pl, lp, want_ctx),
        hgrn2_mixer(pc, pl, lower_bound, lp['hg_norm_w'], want_ctx),
        gla_mixer(pc, pl, lp['gla_wa2'], lp['gla_ba'], lp['gla_norm_w'], want_ctx),
    ]
    y_lat = jnp.concatenate([o[1] for o in outs], axis=-1)
    y_ctx = jnp.concatenate([o[0] for o in outs], axis=-1) if want_ctx else None
    return y_ctx, y_lat


def swiglu(x, w1, w3, w2):
    return (jax.nn.silu(x @ w1) * (x @ w3)) @ w2


def moe_ffn(xf, router_w, router_b, w1, w3, w2):
    n_tok, d = xf.shape
    logits = (xf @ router_w + router_b).astype(jnp.float32)
    top_v, top_i = lax.top_k(logits, TOP_K)
    gates = jax.nn.softmax(top_v, axis=-1).astype(xf.dtype)
    n_assign = n_tok * TOP_K
    flat_e = top_i.reshape(-1)
    flat_tok = jnp.arange(n_assign, dtype=jnp.int32) // TOP_K
    flat_g = gates.reshape(-1)
    order = jnp.argsort(flat_e)
    se, stok, sg = flat_e[order], flat_tok[order], flat_g[order]
    counts = jnp.bincount(flat_e, length=N_EXPERTS)
    starts = jnp.cumsum(counts) - counts
    padded = (counts + MOE_BLOCK - 1) // MOE_BLOCK * MOE_BLOCK
    pends = jnp.cumsum(padded)
    pstarts = pends - padded
    dest = pstarts[se] + (jnp.arange(n_assign, dtype=jnp.int32) - starts[se])
    n_blocks = -(-n_assign // MOE_BLOCK) + N_EXPERTS
    cap = n_blocks * MOE_BLOCK
    xbuf = jnp.zeros((cap, d), xf.dtype).at[dest].set(xf[stok])
    block_start = jnp.arange(n_blocks, dtype=jnp.int32) * MOE_BLOCK
    block_e = jnp.minimum(jnp.searchsorted(pends, block_start, side='right'), N_EXPERTS - 1)

    def expert_block(args):
        xb, e = args
        return swiglu(xb, w1[e], w3[e], w2[e])

    ybuf = lax.map(expert_block, (xbuf.reshape(n_blocks, MOE_BLOCK, d), block_e)).reshape(cap, d)
    return jnp.zeros_like(xf).at[stok].add(ybuf[dest] * sg[:, None])


def setup_inputs(seed: int = 0) -> dict:
    key = jax.random.key(seed)
    ks = iter(jax.random.split(key, 48))
    f32 = jnp.float32

    def nrm(shape, scale):
        return scale * jax.random.normal(next(ks), shape, f32)

    G, P, Cg = S5_GROUPS, S5_STATE, S5_GROUP
    return {
        'x': nrm((BATCH, SEQ, D_MODEL), 1.0),
        'c': nrm((BATCH, D_MODEL), 1.0),
        'ctx': nrm((BATCH, CTX_LEN, D_MODEL), 1.0),
        'c_ctx': nrm((D_MODEL,), 1.0),
        'ada_w': nrm((DEPTH, D_MODEL, 6 * D_MODEL), 0.5 * D_MODEL ** -0.5),
        'ada_b': nrm((DEPTH, 6 * D_MODEL), 0.02),
        'norm1_w': 1.0 + nrm((DEPTH, D_MODEL), 0.02),
        'norm2_w': 1.0 + nrm((DEPTH, D_MODEL), 0.02),
        'w_in': nrm((DEPTH, D_MODEL, D_IN), D_MODEL ** -0.5),
        'w_out': nrm((DEPTH, D_MIX, D_MODEL), D_MIX ** -0.5),
        'ret_gn_w': 1.0 + nrm((DEPTH, RET_HEADS * RET_DV), 0.02),
        's5_lam_re': -0.5 + nrm((DEPTH, 2, G, P), 0.01),
        's5_lam_im': math.pi * jnp.arange(P, dtype=f32) + nrm((DEPTH, 2, G, P), 0.01),
        's5_log_dt': jax.random.uniform(next(ks), (DEPTH, 2, G), f32, math.log(1e-3), math.log(1e-1)),
        's5_b_re': nrm((DEPTH, 2, G, P, Cg), (2 * Cg) ** -0.5),
        's5_b_im': nrm((DEPTH, 2, G, P, Cg), (2 * Cg) ** -0.5),
        's5_c_re': nrm((DEPTH, 2, G, Cg, P), P ** -0.5),
        's5_c_im': nrm((DEPTH, 2, G, Cg, P), P ** -0.5),
        's5_d': nrm((DEPTH, MIXER_W), 1.0),
        's5_glu_w': nrm((DEPTH, MIXER_W, MIXER_W), MIXER_W ** -0.5),
        's5_glu_b': nrm((DEPTH, MIXER_W), 0.02),
        'hg_lb_logits': nrm((DEPTH, HG_HEADS * HG_DK), 0.5),
        'hg_norm_w': 1.0 + nrm((DEPTH, HG_HEADS * HG_DV), 0.02),
        'gla_wa2': nrm((DEPTH, 2, GLA_RANK, GLA_HEADS * GLA_DK), GLA_RANK ** -0.5),
        'gla_ba': nrm((DEPTH, 2, GLA_HEADS * GLA_DK), 0.5),
        'gla_norm_w': 1.0 + nrm((DEPTH, GLA_HEADS * GLA_DV), 0.02),
        'ffn_w1': nrm((N_DENSE, D_MODEL, D_FF), D_MODEL ** -0.5),
        'ffn_w3': nrm((N_DENSE, D_MODEL, D_FF), D_MODEL ** -0.5),
        'ffn_w2': nrm((N_DENSE, D_FF, D_MODEL), D_FF ** -0.5),
        'router_w': nrm((N_MOE, D_MODEL, N_EXPERTS), D_MODEL ** -0.5),
        'router_b': nrm((N_MOE, N_EXPERTS), 0.01),
        'moe_w1': nrm((N_MOE, N_EXPERTS, D_MODEL, D_FF_EXPERT), D_MODEL ** -0.5),
        'moe_w3': nrm((N_MOE, N_EXPERTS, D_MODEL, D_FF_EXPERT), D_MODEL ** -0.5),
        'moe_w2': nrm((N_MOE, N_EXPERTS, D_FF_EXPERT, D_MODEL), D_FF_EXPERT ** -0.5),
        'final_norm_w': 1.0 + nrm((D_MODEL,), 0.02),
    }


def reference(x, c, ctx, c_ctx, ada_w, ada_b, norm1_w, norm2_w, w_in, w_out, ret_gn_w,
              s5_lam_re, s5_lam_im, s5_log_dt, s5_b_re, s5_b_im, s5_c_re, s5_c_im, s5_d,
              s5_glu_w, s5_glu_b, hg_lb_logits, hg_norm_w, gla_wa2, gla_ba, gla_norm_w,
              ffn_w1, ffn_w3, ffn_w2, router_w, router_b, moe_w1, moe_w3, moe_w2, final_norm_w):
    bsz, seq, d = x.shape
    rows = seq // GRID_W
    t_idx = jnp.arange(rows * GRID_W, dtype=jnp.int32)
    row = (t_idx // GRID_W).astype(jnp.float32)
    col = (t_idx % GRID_W).astype(jnp.float32)

    lb_sm = jax.nn.softmax(hg_lb_logits.astype(jnp.float32), axis=0)
    lower_bounds = jnp.clip(jnp.cumsum(lb_sm, axis=0) - lb_sm[0], 0.0, LB_CEIL)

    cond_lat = jax.nn.silu(c)
    cond_ctx = jax.nn.silu(c_ctx)
    h = ctx
    n_lat = bsz * seq
    for l in range(DEPTH):
        keep_ctx = l < DEPTH - 1
        ml = [m[:, None, :] for m in jnp.split(cond_lat @ ada_w[l] + ada_b[l], 6, axis=-1)]
        mc = jnp.split(cond_ctx @ ada_w[l] + ada_b[l], 6, axis=-1)

        pl = split_cols(modulate(rmsnorm(x, norm1_w[l]), ml[0], ml[1]) @ w_in[l])
        pc = split_cols(modulate(rmsnorm(h, norm1_w[l]), mc[0], mc[1]) @ w_in[l])
        lp = {
            'ret_gn_w': ret_gn_w[l],
            's5_lam_re': s5_lam_re[l], 's5_lam_im': s5_lam_im[l], 's5_log_dt': s5_log_dt[l],
            's5_b_re': s5_b_re[l], 's5_b_im': s5_b_im[l], 's5_c_re': s5_c_re[l], 's5_c_im': s5_c_im[l],
            's5_d': s5_d[l], 's5_glu_w': s5_glu_w[l], 's5_glu_b': s5_glu_b[l],
            'hg_norm_w': hg_norm_w[l],
            'gla_wa2': gla_wa2[l], 'gla_ba': gla_ba[l], 'gla_norm_w': gla_norm_w[l],
        }
        y_ctx, y_lat = token_mix(pc, pl, lp, lower_bounds[l], row, col, keep_ctx)
        x = x + ml[2] * (y_lat @ w_out[l])
        if keep_ctx:
            h = h + mc[2] * (y_ctx @ w_out[l])

        tokens = modulate(rmsnorm(x, norm2_w[l]), ml[3], ml[4]).reshape(-1, d)
        if keep_ctx:
            fc = modulate(rmsnorm(h, norm2_w[l]), mc[3], mc[4]).reshape(-1, d)
            tokens = jnp.concatenate([tokens, fc], axis=0)
        j = l // 2
        if l % 2 == 0:
            f = swiglu(tokens, ffn_w1[j], ffn_w3[j], ffn_w2[j])
        else:
            f = moe_ffn(tokens, router_w[j], router_b[j], moe_w1[j], moe_w3[j], moe_w2[j])
        x = x + ml[5] * f[:n_lat].reshape(bsz, seq, d)
        if keep_ctx:
            h = h + mc[5] * f[n_lat:].reshape(h.shape)

    return rmsnorm(x, final_norm_w)
```

```python
import functools
import math

import numpy as np
import jax
import jax.numpy as jnp
from jax import lax
from jax.experimental import pallas as pl
from jax.experimental.pallas import tpu as pltpu

F32 = jnp.float32
BF16 = jnp.bfloat16

D_MODEL = 1024
GRID_W = 64
MIX_W = 256
NORM_EPS = 1e-6
ROPE_BASE = 10000.0
LB_FLOOR = 1e-30
LB_CEIL = 1.0 - 1e-6
N_HEADS = 4
S5_GROUP = 16
S5_GROUPS = 16
S5_STATE = 64
S5_SUB = 16
GLA_RANK = 16
GLA_TAU = 16.0
N_EXPERTS = 8
TOP_K = 2
D_IN = 3360
D_IN_PAD = 3456

SCAN_CHUNK = 256
SCAN_SUB = 16
NEG_BIG = -1e30
VMEM_LIMIT = 56 * 1024 * 1024


def _cparams(sem):
    return pltpu.CompilerParams(dimension_semantics=sem, vmem_limit_bytes=VMEM_LIMIT)


def _full(shape):
    n = len(shape)
    return pl.BlockSpec(shape, lambda *_: (0,) * n)


def _dot(a, b):
    return jnp.dot(a, b, preferred_element_type=F32)


def _dot_nt(a, b):
    return lax.dot_general(a, b, (((1,), (1,)), ((), ())), preferred_element_type=F32)


def _dot_hi(a, b):
    return jnp.dot(a, b, preferred_element_type=F32, precision=lax.Precision.HIGHEST)


def _sigmoid(z):
    return 1.0 / (1.0 + jnp.exp(-z))


def _silu(z):
    return z * _sigmoid(z)


def _log_sigmoid(z):
    return jnp.minimum(z, 0.0) - jnp.log1p(jnp.exp(-jnp.abs(z)))


def _logaddexp(a, b):
    return jnp.maximum(a, b) + jnp.log1p(jnp.exp(-jnp.abs(a - b)))


def _scan_constants(n_heads, wk, wv, reverse):
    c, s = SCAN_CHUNK, SCAN_SUB
    i = np.arange(c)[:, None]
    r = np.arange(c)[None, :]
    cums = []
    if not reverse:
        cums.append(r <= i)
        cums.append(r > i)
    else:
        cums.append(r >= i)
        cums.append(r < i)
    pair_masks = []
    m = s
    while m < c:
        same = (i // (2 * m)) == (r // (2 * m))
        i_hi, r_hi = (i % (2 * m)) >= m, (r % (2 * m)) >= m
        if not reverse:
            later = same & i_hi & r_hi & (r <= i)
            earlier = same & ~i_hi & ~r_hi & (r > i)
            pair = same & i_hi & ~r_hi
        else:
            later = same & ~i_hi & ~r_hi & (r >= i)
            earlier = same & i_hi & r_hi & (r < i)
            pair = same & ~i_hi & r_hi
        cums.append(later | earlier)
        pair_masks.append(pair)
        m *= 2
    dk, dv = wk // n_heads, wv // n_heads
    lk, lv = np.arange(wk), np.arange(wv)
    hmk = (lk[None, :] // dk) == np.arange(n_heads)[:, None]
    hmv = (lv[None, :] // dv) == np.arange(n_heads)[:, None]
    bd = (lv[:, None] // dv) == (lk[None, :] // dk)
    return dict(
        cm=jnp.asarray(np.stack(cums), BF16),
        pm=jnp.asarray(np.stack(pair_masks), F32),
        hmk=jnp.asarray(hmk[:, None, :], F32),
        hmv=jnp.asarray(hmv[:, None, :], F32),
        bd=jnp.asarray(bd, F32),
        gmat=jnp.asarray(bd.T, BF16),
    )


def _split3(g):
    hi = g.astype(BF16)
    r1 = g - hi.astype(F32)
    mid = r1.astype(BF16)
    lo = (r1 - mid.astype(F32)).astype(BF16)
    return hi, mid, lo


def _running(mat, parts):
    return _dot(mat, parts[0]) + _dot(mat, parts[1]) + _dot(mat, parts[2])


def _scan_core(q, k, v, g, cm_ref, pm_ref, hmk_ref, hmv_ref, bd_ref, gmat_ref,
               st_ref, bb_ref, qb_ref, kb_ref, vb_ref, wj_ref, o_ref, *, n_heads, reverse):
    c, s = SCAN_CHUNK, SCAN_SUB
    wk, wv = q.shape[1], v.shape[1]
    n_lev = pm_ref.shape[0]
    parts = _split3(g)
    vb16 = v.astype(BF16)

    b = _running(cm_ref[0], parts)
    after = _running(cm_ref[1], parts)

    st = st_ref[...]
    o = _dot_nt((q * jnp.exp(b)).astype(BF16), st.astype(BF16))
    end = 0 if reverse else c - 1
    upd = _dot(v.T.astype(BF16), (k * jnp.exp(after)).astype(BF16))
    st_ref[...] = st * jnp.exp(b[end:end + 1, :]) + upd * bd_ref[...]

    hmk = hmk_ref[...]
    a = jnp.zeros((n_heads, c, c), F32)
    for lev in range(n_lev):
        ed = jnp.exp(_running(cm_ref[2 + lev], parts))
        qexp = ((q * ed)[None] * hmk).reshape(n_heads * c, wk).astype(BF16)
        sc = _dot_nt(qexp, (k * ed).astype(BF16))
        a = a + sc.reshape(n_heads, c, c) * pm_ref[lev][None]
    pv = _dot(a.reshape(n_heads * c, c).astype(BF16), vb16)
    o = o + jnp.sum(pv.reshape(n_heads, c, wv) * hmv_ref[...], axis=0)

    bb_ref[...] = b
    qb_ref[...] = q
    kb_ref[...] = k
    vb_ref[...] = v
    o_ref[...] = o
    rows = lax.broadcasted_iota(jnp.int32, (s, 1), 0)

    def blk(ib, carry):
        base = pl.multiple_of(ib * s, s)
        bblk = bb_ref[pl.ds(base, s), :]
        qblk = qb_ref[pl.ds(base, s), :]
        for jo in range(s):
            bj = bb_ref[pl.ds(base + jo, 1), :]
            kj = kb_ref[pl.ds(base + jo, 1), :]
            valid = (rows <= jo) if reverse else (rows >= jo)
            w = qblk * kj * jnp.exp(jnp.where(valid, bblk - bj, NEG_BIG))
            wj_ref[pl.ds(jo * s, s), :] = w.astype(BF16)
        r = _dot(wj_ref[...], gmat_ref[...])
        acc = jnp.zeros((s, wv), F32)
        for jo in range(s):
            acc = acc + r[jo * s:(jo + 1) * s, :] * vb_ref[pl.ds(base + jo, 1), :]
        o_ref[pl.ds(base, s), :] += acc
        return carry

    lax.fori_loop(0, c // s, blk, 0)


def _rope(t, cos, sin_signed):
    outs = []
    for h in range(t.shape[1] // 128):
        th = t[:, h * 128:(h + 1) * 128]
        lane = lax.broadcasted_iota(jnp.int32, th.shape, 1)
        partner = jnp.where((lane % 32) < 16, pltpu.roll(th, 128 - 16, 1), pltpu.roll(th, 16, 1))
        outs.append(th * cos[:, h * 128:(h + 1) * 128] + partner * sin_signed[:, h * 128:(h + 1) * 128])
    return jnp.concatenate(outs, axis=1)


def _ret_kernel(p_ref, cos_ref, sin_ref, lg_ref, *rest, reverse):
    consts, (o_ref, st_ref, *scr) = rest[:6], rest[6:]

    @pl.when(pl.program_id(1) == 0)
    def _():
        st_ref[...] = jnp.zeros_like(st_ref)

    cos, sin = cos_ref[...], sin_ref[...]
    q = _rope(p_ref[:, 0:256], cos, sin)
    k = _rope(p_ref[:, 256:512], cos, sin) * (64.0 ** -0.5)
    v = p_ref[:, 512:768]
    g = jnp.broadcast_to(lg_ref[...], q.shape)
    _scan_core(q, k, v, g, *consts, st_ref, *scr, o_ref, n_heads=N_HEADS, reverse=reverse)


def _hg_kernel(p_ref, lb_ref, *rest, reverse):
    consts, (o_ref, st_ref, *scr) = rest[:6], rest[6:]

    @pl.when(pl.program_id(1) == 0)
    def _():
        st_ref[...] = jnp.zeros_like(st_ref)

    q = p_ref[:, 0:256]
    z = p_ref[:, 512:768] if reverse else p_ref[:, 256:512]
    v = p_ref[:, 768:1024]
    log_lb, log_1m_lb, one_m_lb = lb_ref[0:1, :], lb_ref[1:2, :], lb_ref[2:3, :]
    g = _logaddexp(log_lb, log_1m_lb + _log_sigmoid(z))
    k = one_m_lb * _sigmoid(-z)
    _scan_core(q, k, v, g, *consts, st_ref, *scr, o_ref, n_heads=N_HEADS, reverse=reverse)


def _gla_kernel(p_ref, wa_ref, ba_ref, *rest, reverse):
    consts, (o_ref, st_ref, *scr) = rest[:6], rest[6:]

    @pl.when(pl.program_id(1) == 0)
    def _():
        st_ref[...] = jnp.zeros_like(st_ref)

    q = p_ref[:, 0:128]
    k = p_ref[:, 128:256] * (32.0 ** -0.5)
    v = p_ref[:, 256:512]
    g = _log_sigmoid(_dot_hi(p_ref[:, 768:896], wa_ref[...]) + ba_ref[...]) * (1.0 / GLA_TAU)
    _scan_core(q, k, v, g, *consts, st_ref, *scr, o_ref, n_heads=N_HEADS, reverse=reverse)


def _scan_call(kern, p, extra, extra_specs, wk, wv, n_ctx_tiles, reverse):
    bsz, t_all, wp = p.shape
    c, s = SCAN_CHUNK, SCAN_SUB
    n_tiles = t_all // c
    cst = _scan_constants(N_HEADS, wk, wv, reverse)
    if reverse:
        def tile(step):
            return jnp.where(step < n_ctx_tiles, n_ctx_tiles - 1 - step, n_tiles - 1 - (step - n_ctx_tiles))
    else:
        def tile(step):
            return step
    consts = [cst[n] for n in ('cm', 'pm', 'hmk', 'hmv', 'bd', 'gmat')]
    in_specs = [pl.BlockSpec((None, c, wp), lambda b, t: (b, tile(t), 0))]
    in_specs += [spec(tile) for spec in extra_specs]
    in_specs += [_full(x.shape) for x in consts]
    return pl.pallas_call(
        functools.partial(kern, reverse=reverse),
        grid=(bsz, n_tiles),
        in_specs=in_specs,
        out_specs=pl.BlockSpec((None, c, wv), lambda b, t: (b, tile(t), 0)),
        out_shape=jax.ShapeDtypeStruct((bsz, t_all, wv), F32),
        scratch_shapes=[
            pltpu.VMEM((wv, wk), F32),
            pltpu.VMEM((c, wk), F32), pltpu.VMEM((c, wk), F32), pltpu.VMEM((c, wk), F32),
            pltpu.VMEM((c, wv), F32),
            pltpu.VMEM((s * s, wk), BF16),
        ],
        compiler_params=_cparams(("arbitrary", "arbitrary")),
    )(p, *extra, *consts)


def _const_spec(shape):
    return lambda tile: _full(shape)


def _ret_scan(p_ret, cos, sin, lg, n_ctx_tiles, reverse):
    c = SCAN_CHUNK
    row_spec = lambda tile: pl.BlockSpec((c, 256), lambda b, t: (tile(t), 0))
    return _scan_call(_ret_kernel, p_ret, [cos, sin, lg], [row_spec, row_spec, _const_spec(lg.shape)],
                      256, 256, n_ctx_tiles, reverse)


def _hg_scan(p_hg, lb_rows, n_ctx_tiles, reverse):
    return _scan_call(_hg_kernel, p_hg, [lb_rows], [_const_spec(lb_rows.shape)], 256, 256, n_ctx_tiles, reverse)


def _gla_scan(p_gla, wa, ba, n_ctx_tiles, reverse):
    return _scan_call(_gla_kernel, p_gla, [wa, ba], [_const_spec(wa.shape), _const_spec(ba.shape)],
                      128, 256, n_ctx_tiles, reverse)


def _rope_tables(seq, ctx_len):
    t_idx = jnp.arange(seq, dtype=jnp.int32)
    row = (t_idx // GRID_W).astype(F32)
    col = (t_idx % GRID_W).astype(F32)
    freqs = ROPE_BASE ** (-jnp.arange(16, dtype=F32) / 16)
    ang_r = row[:, None] * freqs[None, :]
    ang_c = col[:, None] * freqs[None, :]
    cos = jnp.concatenate([jnp.cos(ang_r)] * 2 + [jnp.cos(ang_c)] * 2, axis=1)
    sin = jnp.concatenate([-jnp.sin(ang_r), jnp.sin(ang_r), -jnp.sin(ang_c), jnp.sin(ang_c)], axis=1)
    cos = jnp.concatenate([jnp.ones((ctx_len, 64), F32), cos], axis=0)
    sin = jnp.concatenate([jnp.zeros((ctx_len, 64), F32), sin], axis=0)
    return jnp.tile(cos, (1, N_HEADS)), jnp.tile(sin, (1, N_HEADS))


def _ret_decay_rows():
    lg = jnp.log1p(-(2.0 ** (-5.0 - jnp.arange(N_HEADS, dtype=F32))))
    fwd = jnp.repeat(lg, 64)[None, :]
    bwd = jnp.repeat(lg[::-1], 64)[None, :]
    return fwd, bwd


def _s5_operators(lam_re, lam_im, log_dt, b_re, b_im, c_re, c_im, reverse):
    n = S5_SUB
    lam = lax.complex(lam_re, lam_im)
    dt = jnp.exp(log_dt)[:, None]
    log_a = lam * dt
    a_bar = jnp.exp(log_a)
    b_bar = ((a_bar - 1) / lam)[..., None] * lax.complex(b_re, b_im)
    c_mat = lax.complex(c_re, c_im)
    step = np.arange(n)[::-1].copy() if reverse else np.arange(n)
    pw = jnp.exp(log_a[:, None, :] * jnp.arange(n + 1, dtype=F32)[None, :, None])
    kern = jnp.einsum('gcp,gtp,gpd->gtcd', c_mat, pw[:, :n], b_bar).real
    lag = step[None, :] - step[:, None]
    m = jnp.where((lag >= 0)[None, :, :, None, None], kern[:, np.maximum(lag, 0)], 0.0)
    m = jnp.transpose(m, (0, 1, 4, 2, 3)).reshape(-1, n * S5_GROUP, n * S5_GROUP)
    qc = pw[:, (n - 1 - step)][:, :, :, None] * b_bar[:, None]
    qc = jnp.transpose(qc, (0, 1, 3, 2)).reshape(-1, n * S5_GROUP, S5_STATE)
    q = jnp.concatenate([qc.real, qc.imag, qc.imag, qc.real], axis=-1)
    z = c_mat[:, None] * pw[:, step + 1][:, :, None, :]
    z = jnp.transpose(z, (0, 3, 1, 2)).reshape(-1, S5_STATE, n * S5_GROUP)
    pm = jnp.concatenate([z.real, -z.imag, jnp.zeros_like(z.real), jnp.zeros_like(z.real)], axis=1)
    a_row = pw[:, n]
    a1 = jnp.concatenate([a_row.real] * 4, axis=-1)
    a2 = jnp.concatenate([-a_row.imag, a_row.imag, a_row.imag, -a_row.imag], axis=-1)
    return m, q, pm, a1, a2


def _s5_in_kernel(u_ref, q_ref, wf_ref, wb_ref):
    w = _dot(u_ref[...].astype(BF16), q_ref[...])
    wf_ref[...] = w[:, :256]
    wb_ref[...] = w[:, 256:]


def _s5_state_kernel(wf_ref, wb_ref, a_ref, hf_ref, hb_ref, sf_ref, sb_ref):
    @pl.when(pl.program_id(0) == 0)
    def _():
        sf_ref[...] = jnp.zeros_like(sf_ref)
        sb_ref[...] = jnp.zeros_like(sb_ref)

    def swap(h):
        return jnp.concatenate([h[:, 128:], h[:, :128]], axis=1)

    n = wf_ref.shape[0]
    hf, hb = sf_ref[...], sb_ref[...]
    a1f, a2f, a1b, a2b = a_ref[0], a_ref[1], a_ref[2], a_ref[3]
    for i in range(n):
        hf_ref[i] = hf
        hf = a1f * hf + a2f * swap(hf) + wf_ref[i]
        j = n - 1 - i
        hb_ref[j] = hb
        hb = a1b * hb + a2b * swap(hb) + wb_ref[j]
    sf_ref[...] = hf
    sb_ref[...] = hb


def _s5_out_kernel(u_ref, m_ref, hf_ref, hb_ref, pf_ref, pb_ref, y_ref):
    y = _dot(u_ref[...].astype(BF16), m_ref[...])
    y = y + _dot(hf_ref[...].astype(BF16), pf_ref[...])
    y = y + _dot(hb_ref[...].astype(BF16), pb_ref[...])
    y_ref[...] = y


def _s5_mix(u, ops_f, ops_b, ctx_len):
    bsz, t_all, _ = u.shape
    n, g = S5_SUB, S5_GROUPS
    n_rows = t_all // n
    blk = 16
    n_steps = n_rows // blk
    n_ctx = (ctx_len // n) // blk
    uf = u.reshape(bsz, n_rows, n, g, S5_GROUP).transpose(0, 3, 1, 2, 4).reshape(bsz, g, n_rows, n * S5_GROUP)
    m = (ops_f[0] + ops_b[0]).astype(BF16)
    q = jnp.concatenate([ops_f[1], ops_b[1]], axis=-1).astype(BF16)
    u_spec = pl.BlockSpec((None, None, n_rows, 256), lambda b, gi: (b, gi, 0, 0))
    col_spec = pl.BlockSpec((n_rows, 256), lambda b, gi: (0, b * g + gi))
    wshape = jax.ShapeDtypeStruct((n_rows, bsz * g * 256), F32)
    wf, wb = pl.pallas_call(
        _s5_in_kernel, grid=(bsz, g),
        in_specs=[u_spec, pl.BlockSpec((None, 256, 512), lambda b, gi: (gi, 0, 0))],
        out_specs=[col_spec, col_spec], out_shape=[wshape, wshape],
        compiler_params=_cparams(("arbitrary", "arbitrary")),
    )(uf, q)

    a = jnp.stack([jnp.tile(x, (bsz, 1)) for x in (ops_f[3], ops_f[4], ops_b[3], ops_b[4])])
    fwd = lambda s: (s, 0, 0)
    bwd = lambda s: (jnp.where(s < n_ctx, n_ctx - 1 - s, n_steps - 1 - (s - n_ctx)), 0, 0)
    st_block = (blk, bsz * g, 256)
    hshape = jax.ShapeDtypeStruct((n_rows, bsz * g, 256), F32)
    hf, hb = pl.pallas_call(
        _s5_state_kernel, grid=(n_steps,),
        in_specs=[pl.BlockSpec(st_block, fwd), pl.BlockSpec(st_block, bwd), _full(a.shape)],
        out_specs=[pl.BlockSpec(st_block, fwd), pl.BlockSpec(st_block, bwd)],
        out_shape=[hshape, hshape],
        scratch_shapes=[pltpu.VMEM((bsz * g, 256), F32), pltpu.VMEM((bsz * g, 256), F32)],
        compiler_params=_cparams(("arbitrary",)),
    )(wf.reshape(n_rows, bsz * g, 256), wb.reshape(n_rows, bsz * g, 256), a)

    mat_spec = pl.BlockSpec((None, 256, 256), lambda b, gi: (gi, 0, 0))
    y = pl.pallas_call(
        _s5_out_kernel, grid=(bsz, g),
        in_specs=[u_spec, mat_spec, col_spec, col_spec, mat_spec, mat_spec],
        out_specs=u_spec,
        out_shape=jax.ShapeDtypeStruct((bsz, g, n_rows, 256), F32),
        compiler_params=_cparams(("arbitrary", "arbitrary")),
    )(uf, m, hf.reshape(n_rows, bsz * g * 256), hb.reshape(n_rows, bsz * g * 256),
      ops_f[2].astype(BF16), ops_b[2].astype(BF16))
    return y.reshape(bsz, g, n_rows, n, S5_GROUP).transpose(0, 2, 3, 1, 4).reshape(bsz, t_all, 256)


ROW_TILE = 256
M_SHIFT1, M_SCALE1, M_GATE1, M_SHIFT2, M_SCALE2, M_GATE2 = range(6)


def _ada_kernel(cond_ref, w_ref, b_ref, o_ref):
    o_ref[...] = _dot_hi(_silu(cond_ref[...]), w_ref[...]) + b_ref[...]


def _modulation(c, c_ctx, ada_w, ada_b):
    depth, d, d6 = ada_w.shape
    bsz = c.shape[0]
    cond = jnp.zeros((8, d), F32).at[:bsz].set(c).at[bsz].set(c_ctx)
    out = pl.pallas_call(
        _ada_kernel, grid=(depth, d6 // d),
        in_specs=[_full((8, d)),
                  pl.BlockSpec((None, d, d), lambda l, j: (l, 0, j)),
                  pl.BlockSpec((None, 1, d), lambda l, j: (l, 0, j))],
        out_specs=pl.BlockSpec((None, 8, d), lambda l, j: (l, 0, j)),
        out_shape=jax.ShapeDtypeStruct((depth, 8, d6), F32),
        compiler_params=_cparams(("arbitrary", "arbitrary")),
    )(cond, ada_w, ada_b[:, None, :])
    out = out.reshape(depth, 8, 6, d)
    lat = out[:, :bsz]
    ctx = jnp.broadcast_to(out[:, bsz][:, None], lat.shape)
    mods = jnp.stack([ctx, lat], axis=2)
    return jnp.pad(mods, ((0, 0), (0, 0), (0, 0), (0, 2), (0, 0)))


def _rms(x, w):
    return x * lax.rsqrt(jnp.mean(x * x, axis=-1, keepdims=True) + NORM_EPS) * w


def _proj_kernel(x_ref, mod_ref, nw_ref, w_ref, ret_ref, s5_ref, hg_ref, gla_ref):
    h = _rms(x_ref[...], nw_ref[...]) * (1.0 + mod_ref[M_SCALE1:M_SCALE1 + 1, :]) + mod_ref[M_SHIFT1:M_SHIFT1 + 1, :]
    p = _dot(h.astype(BF16), w_ref[...])
    ret_ref[...] = p[:, 0:1024]
    s5_ref[...] = p[:, 1024:1280]
    hg_ref[...] = p[:, 1280:2560]
    gla_ref[...] = p[:, 2560:3456]


def _tile_specs(n_ctx_tiles):
    tm = ROW_TILE
    row = lambda w, j=0: pl.BlockSpec((None, tm, w), lambda b, t: (b, t, j))
    mod = pl.BlockSpec((None, None, 8, D_MODEL), lambda b, t: (b, jnp.where(t >= n_ctx_tiles, 1, 0), 0, 0))
    return row, mod


def _project(xs, mods, norm_w, w_pad, n_ctx_tiles):
    bsz, t_all, d = xs.shape
    row, mod = _tile_specs(n_ctx_tiles)
    widths = (1024, 256, 1280, 896)
    return pl.pallas_call(
        _proj_kernel, grid=(bsz, t_all // ROW_TILE),
        in_specs=[row(d), mod, _full((1, d)), _full(w_pad.shape)],
        out_specs=[row(w) for w in widths],
        out_shape=[jax.ShapeDtypeStruct((bsz, t_all, w), F32) for w in widths],
        compiler_params=_cparams(("arbitrary", "arbitrary")),
    )(xs, mods, norm_w[None, :], w_pad)


def _dot3(x, mat):
    hi, mid, lo = _split3(x)
    return _dot(hi, mat) + _dot(mid, mat) + _dot(lo, mat)


def _head_norm(y, gavg, w, center):
    if center:
        y = y - _dot3(y, gavg)
    return y * lax.rsqrt(_dot3(y * y, gavg) + NORM_EPS) * w


def _gelu_tanh(x):
    return 0.5 * x * (1.0 + jnp.tanh(math.sqrt(2.0 / math.pi) * (x + 0.044715 * (x * x * x))))


def _out_kernel(x_ref, mod_ref, gret_ref, yrf_ref, yrb_ref, u_ref, ys5_ref, ghg_ref, yhf_ref, yhb_ref,
                ggl_ref, ygf_ref, ygb_ref, vec_ref, glu_ref, gavg_ref, wo_ref, nw_ref, *rest, with_router):
    if with_router:
        rw_ref, rb_ref, xo_ref, tok_ref, lg_ref = rest
    else:
        xo_ref, tok_ref = rest
    gavg = gavg_ref[...]
    y_ret = _head_norm(yrf_ref[...] + yrb_ref[...], gavg, vec_ref[0:1, :], True) * _silu(gret_ref[...])
    u = u_ref[...]
    z = _gelu_tanh(ys5_ref[...] + vec_ref[3:4, :] * u)
    y_s5 = z * _sigmoid(_dot(z.astype(BF16), glu_ref[...]) + vec_ref[4:5, :])
    y_hg = _head_norm(yhf_ref[...] + yhb_ref[...], gavg, vec_ref[1:2, :], False) * _silu(ghg_ref[...])
    y_gla = _head_norm(ygf_ref[...] + ygb_ref[...], gavg, vec_ref[2:3, :], False) * _silu(ggl_ref[...])
    acc = _dot(y_ret.astype(BF16), wo_ref[0:256, :])
    acc = acc + _dot(y_s5.astype(BF16), wo_ref[256:512, :])
    acc = acc + _dot(y_hg.astype(BF16), wo_ref[512:768, :])
    acc = acc + _dot(y_gla.astype(BF16), wo_ref[768:1024, :])
    x_new = x_ref[...] + mod_ref[M_GATE1:M_GATE1 + 1, :] * acc
    xo_ref[...] = x_new
    tok = _rms(x_new, nw_ref[...]) * (1.0 + mod_ref[M_SCALE2:M_SCALE2 + 1, :]) + mod_ref[M_SHIFT2:M_SHIFT2 + 1, :]
    tok_ref[...] = tok
    if with_router:
        lg_ref[...] = _dot_hi(tok, rw_ref[...]) + rb_ref[...]


def _mix_out(xs, mods, p_ret, yrf, yrb, p_s5, ys5, p_hg, yhf, yhb, p_gla, ygf, ygb,
             vec, glu_w, w_out, norm2_w, router, n_ctx_tiles):
    bsz, t_all, d = xs.shape
    row, mod = _tile_specs(n_ctx_tiles)
    head = np.arange(MIX_W) // (MIX_W // N_HEADS)
    gavg = jnp.asarray((head[:, None] == head[None, :]) / (MIX_W // N_HEADS), BF16)
    ins = [xs, mods, p_ret, yrf, yrb, p_s5, ys5, p_hg, yhf, yhb, p_gla, ygf, ygb,
           vec, glu_w.astype(BF16), gavg, w_out.astype(BF16), norm2_w[None, :]]
    specs = [row(d), mod, row(256, 3), row(256), row(256), row(256), row(256), row(256, 4), row(256), row(256),
             row(256, 2), row(256), row(256), _full(vec.shape), _full(glu_w.shape), _full(gavg.shape),
             _full(w_out.shape), _full((1, d))]
    out_specs = [row(d), row(d)]
    out_shape = [jax.ShapeDtypeStruct((bsz, t_all, d), F32), jax.ShapeDtypeStruct((bsz, t_all, d), F32)]
    if router is not None:
        rw = jnp.pad(router[0], ((0, 0), (0, 128 - N_EXPERTS)))
        rb = jnp.pad(router[1], (0, 128 - N_EXPERTS))[None, :]
        ins += [rw, rb]
        specs += [_full(rw.shape), _full(rb.shape)]
        out_specs.append(row(128))
        out_shape.append(jax.ShapeDtypeStruct((bsz, t_all, 128), F32))
    return pl.pallas_call(
        functools.partial(_out_kernel, with_router=router is not None), grid=(bsz, t_all // ROW_TILE),
        in_specs=specs, out_specs=out_specs, out_shape=out_shape,
        compiler_params=_cparams(("arbitrary", "arbitrary")),
    )(*ins)


FF_SPLIT = 2


def _swiglu(tok16, w1_ref, w3_ref, w2_ref):
    d_ff = w1_ref.shape[-1]
    step = d_ff // FF_SPLIT
    acc = None
    for j in range(FF_SPLIT):
        sl = slice(j * step, (j + 1) * step)
        h = _silu(_dot(tok16, w1_ref[:, sl])) * _dot(tok16, w3_ref[:, sl])
        part = _dot(h.astype(BF16), w2_ref[sl, :])
        acc = part if acc is None else acc + part
    return acc


def _ffn_kernel(x_ref, tok_ref, mod_ref, w1_ref, w3_ref, w2_ref, o_ref):
    f = _swiglu(tok_ref[...].astype(BF16), w1_ref, w3_ref, w2_ref)
    o_ref[...] = x_ref[...] + mod_ref[M_GATE2:M_GATE2 + 1, :] * f


def _dense_ffn(xs, tok, mods, w1, w3, w2, n_ctx_tiles):
    bsz, t_all, d = xs.shape
    row, mod = _tile_specs(n_ctx_tiles)
    once = lambda a: pl.BlockSpec(a.shape, lambda b, t: (0, 0), pipeline_mode=pl.Buffered(1))
    w1, w3, w2 = w1.astype(BF16), w3.astype(BF16), w2.astype(BF16)
    return pl.pallas_call(
        _ffn_kernel, grid=(bsz, t_all // ROW_TILE),
        in_specs=[row(d), row(d), mod, once(w1), once(w3), once(w2)],
        out_specs=row(d), out_shape=jax.ShapeDtypeStruct(xs.shape, F32),
        compiler_params=_cparams(("arbitrary", "arbitrary")),
    )(xs, tok, mods, w1, w3, w2)


MOE_BLOCK = 512


def _routing(logits):
    n = logits.shape[0]
    top_v, top_i = lax.top_k(logits[:, :N_EXPERTS], TOP_K)
    gates = jax.nn.softmax(top_v, axis=-1)
    flat_e = top_i.reshape(-1).astype(jnp.int32)
    onehot = (flat_e[:, None] == jnp.arange(N_EXPERTS, dtype=jnp.int32)[None, :]).astype(jnp.int32)
    csum = jnp.cumsum(onehot, axis=0)
    counts = csum[-1]
    rank = jnp.take_along_axis(csum, flat_e[:, None], axis=1)[:, 0] - 1
    padded = (counts + MOE_BLOCK - 1) // MOE_BLOCK * MOE_BLOCK
    pends = jnp.cumsum(padded)
    dest = (pends - padded)[flat_e] + rank
    n_blocks = -(-(n * TOP_K) // MOE_BLOCK) + N_EXPERTS
    cap = n_blocks * MOE_BLOCK
    flat_tok = jnp.arange(n * TOP_K, dtype=jnp.int32) // TOP_K
    row_tok = jnp.zeros((cap,), jnp.int32).at[dest].set(flat_tok)
    row_gate = jnp.zeros((cap,), F32).at[dest].set(gates.reshape(-1))
    block_start = jnp.arange(n_blocks, dtype=jnp.int32) * MOE_BLOCK
    block_e = jnp.minimum(jnp.searchsorted(pends, block_start, side='right'), N_EXPERTS - 1).astype(jnp.int32)
    return dest.reshape(n, TOP_K), row_tok, row_gate, block_e


def _row_copy(src_hbm, row, dst_ref, sem):
    return pltpu.make_async_copy(src_hbm.at[pl.ds(row, 1), :], dst_ref, sem)


def _moe_kernel(be_ref, idx_ref, idx_next_ref, gate_ref, tok_hbm, w1_ref, w3_ref, w2_ref, o_ref, xg_ref, sem):
    del be_ref
    i, j = pl.program_id(0), pl.program_id(1)
    slot = i % 2
    bm = xg_ref.shape[1]

    def gather(idx, to_slot):
        def body(r, carry):
            _row_copy(tok_hbm, idx[0, r], xg_ref.at[to_slot, pl.ds(r, 1), :], sem.at[to_slot]).start()
            return carry
        lax.fori_loop(0, bm, body, 0)

    @pl.when((i == 0) & (j == 0))
    def _():
        gather(idx_ref, 0)

    @pl.when((j == 0) & (i + 1 < pl.num_programs(0)))
    def _():
        gather(idx_next_ref, 1 - slot)

    @pl.when(j == 0)
    def _():
        def body(r, carry):
            _row_copy(tok_hbm, 0, xg_ref.at[slot, pl.ds(r, 1), :], sem.at[slot]).wait()
            return carry
        lax.fori_loop(0, bm, body, 0)

    x16 = xg_ref[slot].astype(BF16)
    h = _silu(_dot(x16, w1_ref[...])) * _dot(x16, w3_ref[...])
    part = _dot(h.astype(BF16), w2_ref[...]) * gate_ref[...]

    @pl.when(j == 0)
    def _():
        o_ref[...] = part

    @pl.when(j > 0)
    def _():
        o_ref[...] += part


def _moe_experts(tok, row_tok, row_gate, block_e, w1, w3, w2):
    n, d = tok.shape
    bm = MOE_BLOCK
    n_blocks = block_e.shape[0]
    d_ff = w1.shape[-1]
    step = d_ff // FF_SPLIT
    idx = row_tok.reshape(n_blocks, 1, bm)
    smem_blk = lambda f: pl.BlockSpec((None, 1, bm), f, memory_space=pltpu.SMEM)
    grid_spec = pltpu.PrefetchScalarGridSpec(
        num_scalar_prefetch=1, grid=(n_blocks, FF_SPLIT),
        in_specs=[
            smem_blk(lambda i, j, be: (i, 0, 0)),
            smem_blk(lambda i, j, be: (jnp.minimum(i + 1, n_blocks - 1), 0, 0)),
            pl.BlockSpec((bm, 1), lambda i, j, be: (i, 0)),
            pl.BlockSpec(memory_space=pl.ANY),
            pl.BlockSpec((None, d, step), lambda i, j, be: (be[i], 0, j)),
            pl.BlockSpec((None, d, step), lambda i, j, be: (be[i], 0, j)),
            pl.BlockSpec((None, step, d), lambda i, j, be: (be[i], j, 0)),
        ],
        out_specs=pl.BlockSpec((bm, d), lambda i, j, be: (i, 0)),
        scratch_shapes=[pltpu.VMEM((2, bm, d), F32), pltpu.SemaphoreType.DMA((2,))],
    )
    return pl.pallas_call(
        _moe_kernel, grid_spec=grid_spec,
        out_shape=jax.ShapeDtypeStruct((n_blocks * bm, d), F32),
        compiler_params=_cparams(("arbitrary", "arbitrary")),
    )(block_e, idx, idx, row_gate[:, None], tok, w1.astype(BF16), w3.astype(BF16), w2.astype(BF16))


def _combine_kernel(d_ref, dn_ref, x_ref, mod_ref, y_hbm, o_ref, buf_ref, sem):
    i = pl.program_id(0)
    slot = i % 2
    tm = x_ref.shape[0]

    def gather(idx, to_slot):
        def body(r, carry):
            for k in range(TOP_K):
                _row_copy(y_hbm, idx[k, r], buf_ref.at[to_slot, k, pl.ds(r, 1), :], sem.at[to_slot]).start()
            return carry
        lax.fori_loop(0, tm, body, 0)

    @pl.when(i == 0)
    def _():
        gather(d_ref, 0)

    @pl.when(i + 1 < pl.num_programs(0))
    def _():
        gather(dn_ref, 1 - slot)

    def wait_body(r, carry):
        for k in range(TOP_K):
            _row_copy(y_hbm, 0, buf_ref.at[slot, k, pl.ds(r, 1), :], sem.at[slot]).wait()
        return carry
    lax.fori_loop(0, tm, wait_body, 0)

    f = buf_ref[slot, 0] + buf_ref[slot, 1]
    o_ref[...] = x_ref[...] + mod_ref[M_GATE2:M_GATE2 + 1, :] * f


def _moe_combine(xs, mods, ybuf, dest, n_ctx_tiles):
    bsz, t_all, d = xs.shape
    tm = ROW_TILE
    tiles_b = t_all // tm
    n_tiles = bsz * tiles_b
    dt = dest.reshape(n_tiles, tm, TOP_K).transpose(0, 2, 1)
    smem_blk = lambda f: pl.BlockSpec((None, TOP_K, tm), f, memory_space=pltpu.SMEM)
    out = pl.pallas_call(
        _combine_kernel, grid=(n_tiles,),
        in_specs=[
            smem_blk(lambda i: (i, 0, 0)),
            smem_blk(lambda i: (jnp.minimum(i + 1, n_tiles - 1), 0, 0)),
            pl.BlockSpec((tm, d), lambda i: (i, 0)),
            pl.BlockSpec((None, None, 8, d),
                         lambda i: (i // tiles_b, jnp.where(i % tiles_b >= n_ctx_tiles, 1, 0), 0, 0)),
            pl.BlockSpec(memory_space=pl.ANY),
        ],
        out_specs=pl.BlockSpec((tm, d), lambda i: (i, 0)),
        out_shape=jax.ShapeDtypeStruct((bsz * t_all, d), F32),
        scratch_shapes=[pltpu.VMEM((2, TOP_K, tm, d), F32), pltpu.SemaphoreType.DMA((2,))],
        compiler_params=_cparams(("arbitrary",)),
    )(dt, dt, xs.reshape(bsz * t_all, d), mods, ybuf)
    return out.reshape(bsz, t_all, d)


def _final_kernel(x_ref, w_ref, o_ref):
    o_ref[...] = _rms(x_ref[...], w_ref[...])


def _final_norm(xs, w, n_ctx_tiles):
    bsz, t_all, d = xs.shape
    tm = ROW_TILE
    n_lat = t_all // tm - n_ctx_tiles
    return pl.pallas_call(
        _final_kernel, grid=(bsz, n_lat),
        in_specs=[pl.BlockSpec((None, tm, d), lambda b, t: (b, t + n_ctx_tiles, 0)), _full((1, d))],
        out_specs=pl.BlockSpec((None, tm, d), lambda b, t: (b, t, 0)),
        out_shape=jax.ShapeDtypeStruct((bsz, n_lat * tm, d), F32),
        compiler_params=_cparams(("arbitrary", "arbitrary")),
    )(xs, w[None, :])


def kernel(x, c, ctx, c_ctx, ada_w, ada_b, norm1_w, norm2_w, w_in, w_out, ret_gn_w, s5_lam_re, s5_lam_im, s5_log_dt, s5_b_re, s5_b_im, s5_c_re, s5_c_im, s5_d, s5_glu_w, s5_glu_b, hg_lb_logits, hg_norm_w, gla_wa2, gla_ba, gla_norm_w, ffn_w1, ffn_w3, ffn_w2, router_w, router_b, moe_w1, moe_w3, moe_w2, final_norm_w):
    bsz, seq, d = x.shape
    ctx_len = ctx.shape[1]
    depth = ada_w.shape[0]
    assert d == D_MODEL and ctx_len % ROW_TILE == 0 and seq % ROW_TILE == 0
    assert ROW_TILE == SCAN_CHUNK and seq % GRID_W == 0
    n_ctx_tiles = ctx_len // ROW_TILE

    xs = jnp.concatenate([ctx, x], axis=1)
    mods = _modulation(c, c_ctx, ada_w, ada_b)
    cos, sin = _rope_tables(seq, ctx_len)
    lg_f, lg_b = _ret_decay_rows()
    lb_sm = jax.nn.softmax(hg_lb_logits.astype(F32), axis=0)
    lower_bounds = jnp.clip(jnp.cumsum(lb_sm, axis=0) - lb_sm[0], 0.0, LB_CEIL)
    w_in_pad = jnp.pad(w_in, ((0, 0), (0, 0), (0, D_IN_PAD - D_IN))).astype(BF16)
    s5_names = (s5_lam_re, s5_lam_im, s5_log_dt, s5_b_re, s5_b_im, s5_c_re, s5_c_im)

    for l in range(depth):
        p_ret, p_s5, p_hg, p_gla = _project(xs, mods[l], norm1_w[l], w_in_pad[l], n_ctx_tiles)

        yrf = _ret_scan(p_ret, cos, sin, lg_f, n_ctx_tiles, False)
        yrb = _ret_scan(p_ret, cos, sin, lg_b, n_ctx_tiles, True)

        ops_f = _s5_operators(*(a[l, 0] for a in s5_names), reverse=False)
        ops_b = _s5_operators(*(a[l, 1] for a in s5_names), reverse=True)
        ys5 = _s5_mix(p_s5, ops_f, ops_b, ctx_len)

        lb = lower_bounds[l]
        lb_rows = jnp.zeros((8, MIX_W), F32).at[0].set(jnp.log(jnp.maximum(lb, LB_FLOOR)))
        lb_rows = lb_rows.at[1].set(jnp.log1p(-lb)).at[2].set(1.0 - lb)
        yhf = _hg_scan(p_hg, lb_rows, n_ctx_tiles, False)
        yhb = _hg_scan(p_hg, lb_rows, n_ctx_tiles, True)

        ygs = []
        for dr in range(2):
            wa = jnp.zeros((128, 128), F32).at[GLA_RANK * dr:GLA_RANK * (dr + 1)].set(gla_wa2[l, dr])
            ygs.append(_gla_scan(p_gla, wa, gla_ba[l, dr][None, :], n_ctx_tiles, bool(dr)))

        vec = jnp.zeros((8, MIX_W), F32).at[0].set(ret_gn_w[l]).at[1].set(hg_norm_w[l]).at[2].set(gla_norm_w[l])
        vec = vec.at[3].set(s5_d[l]).at[4].set(s5_glu_b[l])
        j = l // 2
        router = (router_w[j], router_b[j]) if l % 2 == 1 else None
        outs = _mix_out(xs, mods[l], p_ret, yrf, yrb, p_s5, ys5, p_hg, yhf, yhb, p_gla, ygs[0], ygs[1],
                        vec, s5_glu_w[l], w_out[l], norm2_w[l], router, n_ctx_tiles)
        if l % 2 == 0:
            x_new, tok = outs
            xs = _dense_ffn(x_new, tok, mods[l], ffn_w1[j], ffn_w3[j], ffn_w2[j], n_ctx_tiles)
        else:
            x_new, tok, logits = outs
            n_tok = bsz * xs.shape[1]
            dest, row_tok, row_gate, block_e = _routing(logits.reshape(n_tok, 128))
            ybuf = _moe_experts(tok.reshape(n_tok, d), row_tok, row_gate, block_e, moe_w1[j], moe_w3[j], moe_w2[j])
            xs = _moe_combine(x_new, mods[l], ybuf, dest, n_ctx_tiles)

    return _final_norm(xs, final_norm_w, n_ctx_tiles)
```

```python
import functools
import math

import numpy as np
import jax
import jax.numpy as jnp
from jax import lax
from jax.experimental import pallas as pl
from jax.experimental.pallas import tpu as pltpu

F32 = jnp.float32
BF16 = jnp.bfloat16

D_MODEL = 1024
GRID_W = 64
MIX_W = 256
NORM_EPS = 1e-6
ROPE_BASE = 10000.0
LB_FLOOR = 1e-30
LB_CEIL = 1.0 - 1e-6
N_HEADS = 4
S5_GROUP = 16
S5_GROUPS = 16
S5_STATE = 64
S5_SUB = 16
GLA_RANK = 16
GLA_TAU = 16.0
N_EXPERTS = 8
TOP_K = 2
D_IN = 3360
D_IN_PAD = 3456

SCAN_CHUNK = 256
SCAN_SUB = 16
TAME_LIMIT = 40.0
NEG_BIG = -1e30
VMEM_LIMIT = 56 * 1024 * 1024


def _cparams(sem):
    return pltpu.CompilerParams(dimension_semantics=sem, vmem_limit_bytes=VMEM_LIMIT)


def _full(shape):
    n = len(shape)
    return pl.BlockSpec(shape, lambda *_: (0,) * n)


def _dot(a, b):
    return jnp.dot(a, b, preferred_element_type=F32)


def _dot_nt(a, b):
    return lax.dot_general(a, b, (((1,), (1,)), ((), ())), preferred_element_type=F32)


def _dot_hi(a, b):
    return jnp.dot(a, b, preferred_element_type=F32, precision=lax.Precision.HIGHEST)


def _sigmoid(z):
    return 1.0 / (1.0 + jnp.exp(-z))


def _silu(z):
    return z * _sigmoid(z)


def _log_sigmoid(z):
    return jnp.minimum(z, 0.0) - jnp.log1p(jnp.exp(-jnp.abs(z)))


def _logaddexp(a, b):
    return jnp.maximum(a, b) + jnp.log1p(jnp.exp(-jnp.abs(a - b)))


def _scan_constants(n_heads, wk, wv, reverse):
    c, s, hc = SCAN_CHUNK, SCAN_SUB, SCAN_CHUNK // 2
    i = np.arange(c)[:, None]
    r = np.arange(c)[None, :]
    cum = (r >= i) if reverse else (r <= i)
    ih = np.arange(hc)[:, None]
    jh = np.arange(hc)[None, :]
    causal = (jh >= ih) if reverse else (jh <= ih)
    masks = [((ih // s) == (jh // s)) & causal]
    m = s
    while m < hc:
        same = (ih // (2 * m)) == (jh // (2 * m))
        i_hi, j_hi = (ih % (2 * m)) >= m, (jh % (2 * m)) >= m
        masks.append(same & (~i_hi & j_hi if reverse else i_hi & ~j_hi))
        m *= 2
    dk, dv = wk // n_heads, wv // n_heads
    lk, lv = np.arange(wk), np.arange(wv)
    hmk = (lk[None, :] // dk) == np.arange(n_heads)[:, None]
    hmv = (lv[None, :] // dv) == np.arange(n_heads)[:, None]
    bd = (lv[:, None] // dv) == (lk[None, :] // dk)
    return dict(
        cm=jnp.asarray(np.concatenate([cum] * 2, axis=1), BF16),
        pm=jnp.asarray(np.stack([np.tile(x, (1, n_heads)) for x in masks]), F32),
        hmk=jnp.asarray(hmk[:, None, :], BF16),
        hmv=jnp.asarray(hmv[:, None, :], BF16),
        bd=jnp.asarray(bd, F32),
        gmat=jnp.asarray(bd.T, BF16),
    )


def _split3(g):
    hi = g.astype(BF16)
    r1 = g - hi.astype(F32)
    mid = r1.astype(BF16)
    lo = (r1 - mid.astype(F32)).astype(BF16)
    return hi, mid, lo


def _ref_rows(bb_ref, rows, span):
    wk = bb_ref.shape[1]
    pieces = [jnp.zeros((span, wk), F32) if r is None else jnp.broadcast_to(bb_ref[r:r + 1, :], (span, wk))
              for r in rows]
    return pieces[0] if len(pieces) == 1 else jnp.concatenate(pieces, axis=0)


def _scan_core(q, k, v, g, cm_ref, pm_ref, hmk_ref, hmv_ref, bd_ref, gmat_ref,
               st_ref, bb_ref, qb_ref, kb_ref, vb_ref, wj_ref, o_ref, *, n_heads, reverse):
    c, s, hc = SCAN_CHUNK, SCAN_SUB, SCAN_CHUNK // 2
    wk, wv = q.shape[1], v.shape[1]
    hmk, hmv = hmk_ref[...], hmv_ref[...]
    expand = _expand_heads
    g_hi = g.astype(BF16)
    g_lo = (g - g_hi.astype(F32)).astype(BF16)
    b = _dot(cm_ref[...], jnp.concatenate([g_hi, g_lo], axis=0))
    bb_ref[...] = b
    end = 0 if reverse else c - 1
    b_end = bb_ref[end:end + 1, :]
    st = st_ref[...]
    o_inter = _dot_nt((q * jnp.exp(b)).astype(BF16), st.astype(BF16))
    upd = _dot(v.T.astype(BF16), (k * jnp.exp(b_end - b)).astype(BF16))
    st_ref[...] = st * jnp.exp(b_end) + upd * bd_ref[...]

    v16 = v.astype(BF16)
    vexp = [expand(v16[h0:h0 + hc], hmv) for h0 in (0, hc)]
    halves = (slice(0, hc), slice(hc, c))

    n_blk = c // s
    if reverse:
        rows = [(i + 1) * s for i in range(n_blk - 1)] + [None]
    else:
        rows = [None] + [i * s - 1 for i in range(1, n_blk)]
    dq = b - _ref_rows(bb_ref, rows, s)
    low = jnp.min(jnp.min(dq, axis=1, keepdims=True), axis=0, keepdims=True)
    q16 = (q * jnp.exp(dq)).astype(BF16)
    k16 = (k * jnp.exp(jnp.minimum(-dq, TAME_LIMIT))).astype(BF16)
    pm_diag = pm_ref[0] * (low >= -TAME_LIMIT).astype(F32)
    a = [_dot_nt(q16[h], expand(k16[h], hmk)) * pm_diag for h in halves]

    m, lev = s, 1
    while m < hc:
        n_pair = c // (2 * m)
        rows = [p * 2 * m + (m if reverse else m - 1) for p in range(n_pair)]
        ed = jnp.exp(-jnp.abs(b - _ref_rows(bb_ref, rows, 2 * m)))
        q16, k16 = (q * ed).astype(BF16), (k * ed).astype(BF16)
        a = [a[i] + _dot_nt(q16[h], expand(k16[h], hmk)) * pm_ref[lev] for i, h in enumerate(halves)]
        m, lev = 2 * m, lev + 1

    ed = jnp.exp(-jnp.abs(b - _ref_rows(bb_ref, [hc if reverse else hc - 1], c)))
    q16, k16 = (q * ed).astype(BF16), (k * ed).astype(BF16)
    late, early = (0, 1) if reverse else (1, 0)
    top = _dot_nt(q16[halves[late]], expand(k16[halves[early]], hmk))

    o_ref[halves[early], :] = o_inter[halves[early]] + _dot(a[early].astype(BF16), vexp[early])
    a_late = jnp.concatenate([a[late], top], axis=1).astype(BF16)
    o_ref[halves[late], :] = o_inter[halves[late]] + _dot(a_late, jnp.concatenate([vexp[late], vexp[early]], axis=0))

    @pl.when(low[0, 0] < -TAME_LIMIT)
    def _():
        _explicit_diagonal(q, k, v, gmat_ref, bb_ref, qb_ref, kb_ref, vb_ref, wj_ref, o_ref, reverse)


def _explicit_diagonal(q, k, v, gmat_ref, bb_ref, qb_ref, kb_ref, vb_ref, wj_ref, o_ref, reverse):
    c, s = SCAN_CHUNK, SCAN_SUB
    wv = v.shape[1]
    qb_ref[...] = q
    kb_ref[...] = k
    vb_ref[...] = v
    rows = lax.broadcasted_iota(jnp.int32, (s, 1), 0)

    def blk(ib, carry):
        base = pl.multiple_of(ib * s, s)
        bblk = bb_ref[pl.ds(base, s), :]
        qblk = qb_ref[pl.ds(base, s), :]
        for jo in range(s):
            bj = bb_ref[pl.ds(base + jo, 1), :]
            kj = kb_ref[pl.ds(base + jo, 1), :]
            valid = (rows <= jo) if reverse else (rows >= jo)
            w = qblk * kj * jnp.exp(jnp.where(valid, bblk - bj, NEG_BIG))
            wj_ref[pl.ds(jo * s, s), :] = w.astype(BF16)
        r = _dot(wj_ref[...], gmat_ref[...])
        acc = jnp.zeros((s, wv), F32)
        for jo in range(s):
            acc = acc + r[jo * s:(jo + 1) * s, :] * vb_ref[pl.ds(base + jo, 1), :]
        o_ref[pl.ds(base, s), :] += acc
        return carry

    lax.fori_loop(0, c // s, blk, 0)


def _rope(t, cos, sin_signed):
    outs = []
    for h in range(t.shape[1] // 128):
        th = t[:, h * 128:(h + 1) * 128]
        lane = lax.broadcasted_iota(jnp.int32, th.shape, 1)
        partner = jnp.where((lane % 32) < 16, pltpu.roll(th, 128 - 16, 1), pltpu.roll(th, 16, 1))
        outs.append(th * cos[:, h * 128:(h + 1) * 128] + partner * sin_signed[:, h * 128:(h + 1) * 128])
    return jnp.concatenate(outs, axis=1)


def _expand_heads(x16, hm):
    return (x16[None] * hm).reshape(hm.shape[0] * x16.shape[0], x16.shape[1])


def _ret_constants(reverse):
    c, hc = SCAN_CHUNK, SCAN_CHUNK // 2
    lg = jnp.log1p(-(2.0 ** (-5.0 - jnp.arange(N_HEADS, dtype=F32))))
    lg = lg[::-1] if reverse else lg
    lane = jnp.repeat(lg, MIX_W // N_HEADS)[None, :]
    i = jnp.arange(c, dtype=F32)[:, None]
    steps_in = (c - i) if reverse else (i + 1.0)
    steps_out = i if reverse else (c - 1.0 - i)
    dec = jnp.concatenate([jnp.exp(lane * steps_in), jnp.exp(lane * steps_out),
                           jnp.broadcast_to(jnp.exp(lane * c), (8, MIX_W))], axis=0)
    ih = jnp.arange(hc, dtype=F32)[:, None]
    jh = jnp.arange(hc, dtype=F32)[None, :]
    dist = (jh - ih) if reverse else (ih - jh)
    per_head = lambda d: jnp.concatenate(
        [jnp.where(d >= 0, jnp.exp(lg[h] * jnp.maximum(d, 0.0)), 0.0) for h in range(N_HEADS)], axis=1)
    dm = jnp.stack([per_head(dist), per_head(dist + hc)])
    hd = np.arange(MIX_W) // (MIX_W // N_HEADS)
    hm = jnp.asarray((hd[None, :] == np.arange(N_HEADS)[:, None])[:, None, :], BF16)
    bd = jnp.asarray(hd[:, None] == hd[None, :], F32)
    return dec, dm, hm, bd


def _ret_kernel(p_ref, cos_ref, sin_ref, dec_ref, dm_ref, hm_ref, bd_ref, o_ref, st_ref, *, reverse):
    @pl.when(pl.program_id(1) == 0)
    def _():
        st_ref[...] = jnp.zeros_like(st_ref)

    c, hc = SCAN_CHUNK, SCAN_CHUNK // 2
    cos, sin = cos_ref[...], sin_ref[...]
    q = _rope(p_ref[:, 0:256], cos, sin)
    k = _rope(p_ref[:, 256:512], cos, sin) * (64.0 ** -0.5)
    v = p_ref[:, 512:768]
    st = st_ref[...]
    o_inter = _dot_nt((q * dec_ref[0:c, :]).astype(BF16), st.astype(BF16))
    upd = _dot(v.T.astype(BF16), (k * dec_ref[c:2 * c, :]).astype(BF16))
    st_ref[...] = st * dec_ref[2 * c:2 * c + 1, :] + upd * bd_ref[...]

    hm = hm_ref[...]
    q16, k16, v16 = q.astype(BF16), k.astype(BF16), v.astype(BF16)
    halves = (slice(0, hc), slice(hc, c))
    late, early = (0, 1) if reverse else (1, 0)
    kx = [_expand_heads(k16[h], hm) for h in halves]
    vx = [_expand_heads(v16[h], hm) for h in halves]
    a_early = _dot_nt(q16[halves[early]], kx[early]) * dm_ref[0]
    a_late = _dot_nt(q16[halves[late]], kx[late]) * dm_ref[0]
    top = _dot_nt(q16[halves[late]], kx[early]) * dm_ref[1]
    o_ref[halves[early], :] = o_inter[halves[early]] + _dot(a_early.astype(BF16), vx[early])
    a_late = jnp.concatenate([a_late, top], axis=1).astype(BF16)
    o_ref[halves[late], :] = o_inter[halves[late]] + _dot(a_late, jnp.concatenate([vx[late], vx[early]], axis=0))


def _hg_kernel(p_ref, lb_ref, *rest, reverse):
    consts, (o_ref, st_ref, *scr) = rest[:6], rest[6:]

    @pl.when(pl.program_id(1) == 0)
    def _():
        st_ref[...] = jnp.zeros_like(st_ref)

    q = p_ref[:, 0:256]
    z = p_ref[:, 512:768] if reverse else p_ref[:, 256:512]
    v = p_ref[:, 768:1024]
    lb_floor, one_m_lb = lb_ref[0:1, :], lb_ref[1:2, :]
    sg = _sigmoid(z)
    g = jnp.log(lb_floor + one_m_lb * sg)
    k = one_m_lb * (1.0 - sg)
    _scan_core(q, k, v, g, *consts, st_ref, *scr, o_ref, n_heads=N_HEADS, reverse=reverse)


def _gla_kernel(p_ref, wa_ref, ba_ref, *rest, reverse):
    consts, (o_ref, st_ref, *scr) = rest[:6], rest[6:]

    @pl.when(pl.program_id(1) == 0)
    def _():
        st_ref[...] = jnp.zeros_like(st_ref)

    q = p_ref[:, 0:128]
    k = p_ref[:, 128:256] * (32.0 ** -0.5)
    v = p_ref[:, 256:512]
    g = _log_sigmoid(_dot_hi(p_ref[:, 768:896], wa_ref[...]) + ba_ref[...]) * (1.0 / GLA_TAU)
    _scan_core(q, k, v, g, *consts, st_ref, *scr, o_ref, n_heads=N_HEADS, reverse=reverse)


def _scan_call(kern, p, extra, extra_specs, wk, wv, n_ctx_tiles, reverse):
    bsz, t_all, wp = p.shape
    c, s = SCAN_CHUNK, SCAN_SUB
    n_tiles = t_all // c
    cst = _scan_constants(N_HEADS, wk, wv, reverse)
    tile = _scan_tile_map(n_tiles, n_ctx_tiles, reverse)
    consts = [cst[n] for n in ('cm', 'pm', 'hmk', 'hmv', 'bd', 'gmat')]
    in_specs = [pl.BlockSpec((None, c, wp), lambda b, t: (b, tile(t), 0))]
    in_specs += [spec(tile) for spec in extra_specs]
    in_specs += [_full(x.shape) for x in consts]
    return pl.pallas_call(
        functools.partial(kern, reverse=reverse),
        grid=(bsz, n_tiles),
        in_specs=in_specs,
        out_specs=pl.BlockSpec((None, c, wv), lambda b, t: (b, tile(t), 0)),
        out_shape=jax.ShapeDtypeStruct((bsz, t_all, wv), F32),
        scratch_shapes=[
            pltpu.VMEM((wv, wk), F32),
            pltpu.VMEM((c, wk), F32), pltpu.VMEM((c, wk), F32), pltpu.VMEM((c, wk), F32),
            pltpu.VMEM((c, wv), F32),
            pltpu.VMEM((s * s, wk), BF16),
        ],
        compiler_params=_cparams(("arbitrary", "arbitrary")),
    )(p, *extra, *consts)


def _const_spec(shape):
    return lambda tile: _full(shape)


def _scan_tile_map(n_tiles, n_ctx_tiles, reverse):
    if not reverse:
        return lambda step: step
    return lambda step: jnp.where(step < n_ctx_tiles, n_ctx_tiles - 1 - step, n_tiles - 1 - (step - n_ctx_tiles))


def _ret_scan(p_ret, cos, sin, n_ctx_tiles, reverse):
    bsz, t_all, wp = p_ret.shape
    c = SCAN_CHUNK
    tile = _scan_tile_map(t_all // c, n_ctx_tiles, reverse)
    consts = _ret_constants(reverse)
    rows = lambda w: pl.BlockSpec((c, w), lambda b, t: (tile(t), 0))
    return pl.pallas_call(
        functools.partial(_ret_kernel, reverse=reverse),
        grid=(bsz, t_all // c),
        in_specs=[pl.BlockSpec((None, c, wp), lambda b, t: (b, tile(t), 0)), rows(MIX_W), rows(MIX_W)]
        + [_full(x.shape) for x in consts],
        out_specs=pl.BlockSpec((None, c, MIX_W), lambda b, t: (b, tile(t), 0)),
        out_shape=jax.ShapeDtypeStruct((bsz, t_all, MIX_W), F32),
        scratch_shapes=[pltpu.VMEM((MIX_W, MIX_W), F32)],
        compiler_params=_cparams(("arbitrary", "arbitrary")),
    )(p_ret, cos, sin, *consts)


def _hg_scan(p_hg, lb_rows, n_ctx_tiles, reverse):
    return _scan_call(_hg_kernel, p_hg, [lb_rows], [_const_spec(lb_rows.shape)], 256, 256, n_ctx_tiles, reverse)


def _gla_scan(p_gla, wa, ba, n_ctx_tiles, reverse):
    return _scan_call(_gla_kernel, p_gla, [wa, ba], [_const_spec(wa.shape), _const_spec(ba.shape)],
                      128, 256, n_ctx_tiles, reverse)


def _rope_tables(seq, ctx_len):
    t_idx = jnp.arange(seq, dtype=jnp.int32)
    row = (t_idx // GRID_W).astype(F32)
    col = (t_idx % GRID_W).astype(F32)
    freqs = ROPE_BASE ** (-jnp.arange(16, dtype=F32) / 16)
    ang_r = row[:, None] * freqs[None, :]
    ang_c = col[:, None] * freqs[None, :]
    cos = jnp.concatenate([jnp.cos(ang_r)] * 2 + [jnp.cos(ang_c)] * 2, axis=1)
    sin = jnp.concatenate([-jnp.sin(ang_r), jnp.sin(ang_r), -jnp.sin(ang_c), jnp.sin(ang_c)], axis=1)
    cos = jnp.concatenate([jnp.ones((ctx_len, 64), F32), cos], axis=0)
    sin = jnp.concatenate([jnp.zeros((ctx_len, 64), F32), sin], axis=0)
    return jnp.tile(cos, (1, N_HEADS)), jnp.tile(sin, (1, N_HEADS))


def _s5_operators(lam_re, lam_im, log_dt, b_re, b_im, c_re, c_im, reverse):
    n = S5_SUB
    lam = lax.complex(lam_re, lam_im)
    dt = jnp.exp(log_dt)[:, None]
    log_a = lam * dt
    a_bar = jnp.exp(log_a)
    b_bar = ((a_bar - 1) / lam)[..., None] * lax.complex(b_re, b_im)
    c_mat = lax.complex(c_re, c_im)
    step = np.arange(n)[::-1].copy() if reverse else np.arange(n)
    pw = jnp.exp(log_a[:, None, :] * jnp.arange(n + 1, dtype=F32)[None, :, None])
    kern = jnp.einsum('gcp,gtp,gpd->gtcd', c_mat, pw[:, :n], b_bar).real
    lag = step[None, :] - step[:, None]
    m = jnp.where((lag >= 0)[None, :, :, None, None], kern[:, np.maximum(lag, 0)], 0.0)
    m = jnp.transpose(m, (0, 1, 4, 2, 3)).reshape(-1, n * S5_GROUP, n * S5_GROUP)
    qc = pw[:, (n - 1 - step)][:, :, :, None] * b_bar[:, None]
    qc = jnp.transpose(qc, (0, 1, 3, 2)).reshape(-1, n * S5_GROUP, S5_STATE)
    q = jnp.concatenate([qc.real, qc.imag, qc.imag, qc.real], axis=-1)
    z = c_mat[:, None] * pw[:, step + 1][:, :, None, :]
    z = jnp.transpose(z, (0, 3, 1, 2)).reshape(-1, S5_STATE, n * S5_GROUP)
    pm = jnp.concatenate([z.real, -z.imag, jnp.zeros_like(z.real), jnp.zeros_like(z.real)], axis=1)
    a_row = pw[:, n]
    a1 = jnp.concatenate([a_row.real] * 4, axis=-1)
    a2 = jnp.concatenate([-a_row.imag, a_row.imag, a_row.imag, -a_row.imag], axis=-1)
    return m, q, pm, a1, a2


S5_GROUP_BLOCK = 8


def _s5_in_kernel(u_ref, q_ref, wf_ref, wb_ref):
    for k in range(S5_GROUP_BLOCK):
        w = _dot(u_ref[k], q_ref[k])
        wf_ref[:, k, :] = w[:, :256]
        wb_ref[:, k, :] = w[:, 256:]


def _s5_state_kernel(wf_ref, wb_ref, a_ref, hf_ref, hb_ref, sf_ref, sb_ref):
    @pl.when(pl.program_id(0) == 0)
    def _():
        sf_ref[...] = jnp.zeros_like(sf_ref)
        sb_ref[...] = jnp.zeros_like(sb_ref)

    def swap(h):
        return jnp.concatenate([h[:, 128:], h[:, :128]], axis=1)

    n = wf_ref.shape[0]
    hf, hb = sf_ref[...], sb_ref[...]
    a1f, a2f, a1b, a2b = a_ref[0], a_ref[1], a_ref[2], a_ref[3]
    for i in range(n):
        hf_ref[i] = hf
        hf = a1f * hf + a2f * swap(hf) + wf_ref[i]
        j = n - 1 - i
        hb_ref[j] = hb
        hb = a1b * hb + a2b * swap(hb) + wb_ref[j]
    sf_ref[...] = hf
    sb_ref[...] = hb


def _s5_out_kernel(u_ref, m_ref, hf_ref, hb_ref, pf_ref, pb_ref, y_ref):
    for k in range(S5_GROUP_BLOCK):
        y = _dot(u_ref[k], m_ref[k])
        y = y + _dot(hf_ref[:, k, :].astype(BF16), pf_ref[k])
        y = y + _dot(hb_ref[:, k, :].astype(BF16), pb_ref[k])
        y_ref[k] = y


def _s5_mix(u, ops_f, ops_b, ctx_len):
    bsz, t_all, _ = u.shape
    n, g = S5_SUB, S5_GROUPS
    n_rows = t_all // n
    blk = 16
    n_steps = n_rows // blk
    n_ctx = (ctx_len // n) // blk
    gb = S5_GROUP_BLOCK
    n_gb = g // gb
    uf = u.astype(BF16).reshape(bsz, n_rows, n, g, S5_GROUP).transpose(0, 3, 1, 2, 4)
    uf = uf.reshape(bsz, g, n_rows, n * S5_GROUP)
    m = (ops_f[0] + ops_b[0]).astype(BF16)
    q = jnp.concatenate([ops_f[1], ops_b[1]], axis=-1).astype(BF16)
    u_spec = pl.BlockSpec((None, gb, n_rows, 256), lambda b, gi: (b, gi, 0, 0))
    st_spec = pl.BlockSpec((n_rows, gb, 256), lambda b, gi: (0, b * n_gb + gi, 0))
    hshape = jax.ShapeDtypeStruct((n_rows, bsz * g, 256), F32)
    wf, wb = pl.pallas_call(
        _s5_in_kernel, grid=(bsz, n_gb),
        in_specs=[u_spec, pl.BlockSpec((gb, 256, 512), lambda b, gi: (gi, 0, 0))],
        out_specs=[st_spec, st_spec], out_shape=[hshape, hshape],
        compiler_params=_cparams(("arbitrary", "arbitrary")),
    )(uf, q)

    a = jnp.stack([jnp.tile(x, (bsz, 1)) for x in (ops_f[3], ops_f[4], ops_b[3], ops_b[4])])
    fwd = lambda s: (s, 0, 0)
    bwd = lambda s: (jnp.where(s < n_ctx, n_ctx - 1 - s, n_steps - 1 - (s - n_ctx)), 0, 0)
    st_block = (blk, bsz * g, 256)
    hf, hb = pl.pallas_call(
        _s5_state_kernel, grid=(n_steps,),
        in_specs=[pl.BlockSpec(st_block, fwd), pl.BlockSpec(st_block, bwd), _full(a.shape)],
        out_specs=[pl.BlockSpec(st_block, fwd), pl.BlockSpec(st_block, bwd)],
        out_shape=[hshape, hshape],
        scratch_shapes=[pltpu.VMEM((bsz * g, 256), F32), pltpu.VMEM((bsz * g, 256), F32)],
        compiler_params=_cparams(("arbitrary",)),
    )(wf, wb, a)

    mat_spec = pl.BlockSpec((gb, 256, 256), lambda b, gi: (gi, 0, 0))
    y = pl.pallas_call(
        _s5_out_kernel, grid=(bsz, n_gb),
        in_specs=[u_spec, mat_spec, st_spec, st_spec, mat_spec, mat_spec],
        out_specs=u_spec,
        out_shape=jax.ShapeDtypeStruct((bsz, g, n_rows, 256), F32),
        compiler_params=_cparams(("arbitrary", "arbitrary")),
    )(uf, m, hf, hb, ops_f[2].astype(BF16), ops_b[2].astype(BF16))
    return y.reshape(bsz, g, n_rows, n, S5_GROUP).transpose(0, 2, 3, 1, 4).reshape(bsz, t_all, 256)


ROW_TILE = 256
M_SHIFT1, M_SCALE1, M_GATE1, M_SHIFT2, M_SCALE2, M_GATE2 = range(6)


def _ada_kernel(cond_ref, w_ref, b_ref, o_ref):
    o_ref[...] = _dot_hi(_silu(cond_ref[...]), w_ref[...]) + b_ref[...]


def _modulation(c, c_ctx, ada_w, ada_b):
    depth, d, d6 = ada_w.shape
    bsz = c.shape[0]
    cond = jnp.zeros((8, d), F32).at[:bsz].set(c).at[bsz].set(c_ctx)
    out = pl.pallas_call(
        _ada_kernel, grid=(depth, d6 // d),
        in_specs=[_full((8, d)),
                  pl.BlockSpec((None, d, d), lambda l, j: (l, 0, j)),
                  pl.BlockSpec((None, 1, d), lambda l, j: (l, 0, j))],
        out_specs=pl.BlockSpec((None, 8, d), lambda l, j: (l, 0, j)),
        out_shape=jax.ShapeDtypeStruct((depth, 8, d6), F32),
        compiler_params=_cparams(("arbitrary", "arbitrary")),
    )(cond, ada_w, ada_b[:, None, :])
    out = out.reshape(depth, 8, 6, d)
    lat = out[:, :bsz]
    ctx = jnp.broadcast_to(out[:, bsz][:, None], lat.shape)
    mods = jnp.stack([ctx, lat], axis=2)
    return jnp.pad(mods, ((0, 0), (0, 0), (0, 0), (0, 2), (0, 0)))


def _rms(x, w):
    return x * lax.rsqrt(jnp.mean(x * x, axis=-1, keepdims=True) + NORM_EPS) * w


def _proj_kernel(x_ref, mod_ref, nw_ref, w_ref, ret_ref, s5_ref, hg_ref, gla_ref):
    h = _rms(x_ref[...], nw_ref[...]) * (1.0 + mod_ref[M_SCALE1:M_SCALE1 + 1, :]) + mod_ref[M_SHIFT1:M_SHIFT1 + 1, :]
    p = _dot(h.astype(BF16), w_ref[...])
    ret_ref[...] = p[:, 0:1024]
    s5_ref[...] = p[:, 1024:1280]
    hg_ref[...] = p[:, 1280:2560]
    gla_ref[...] = p[:, 2560:3456]


def _tile_specs(n_ctx_tiles):
    tm = ROW_TILE
    row = lambda w, j=0: pl.BlockSpec((None, tm, w), lambda b, t: (b, t, j))
    mod = pl.BlockSpec((None, None, 8, D_MODEL), lambda b, t: (b, jnp.where(t >= n_ctx_tiles, 1, 0), 0, 0))
    return row, mod


def _project(xs, mods, norm_w, w_pad, n_ctx_tiles):
    bsz, t_all, d = xs.shape
    row, mod = _tile_specs(n_ctx_tiles)
    widths = (1024, 256, 1280, 896)
    return pl.pallas_call(
        _proj_kernel, grid=(bsz, t_all // ROW_TILE),
        in_specs=[row(d), mod, _full((1, d)), _full(w_pad.shape)],
        out_specs=[row(w) for w in widths],
        out_shape=[jax.ShapeDtypeStruct((bsz, t_all, w), F32) for w in widths],
        compiler_params=_cparams(("arbitrary", "arbitrary")),
    )(xs, mods, norm_w[None, :], w_pad)


def _dot3(x, mat):
    hi, mid, lo = _split3(x)
    return _dot(hi, mat) + _dot(mid, mat) + _dot(lo, mat)


def _head_norm(y, gavg, w, center):
    if center:
        y = y - _dot3(y, gavg)
    return y * lax.rsqrt(_dot3(y * y, gavg) + NORM_EPS) * w


def _gelu_tanh(x):
    return 0.5 * x * (1.0 + jnp.tanh(math.sqrt(2.0 / math.pi) * (x + 0.044715 * (x * x * x))))


def _out_kernel(x_ref, mod_ref, gret_ref, yrf_ref, yrb_ref, u_ref, ys5_ref, ghg_ref, yhf_ref, yhb_ref,
                ggl_ref, ygf_ref, ygb_ref, vec_ref, glu_ref, gavg_ref, wo_ref, nw_ref, *rest, with_router):
    if with_router:
        rw_ref, rb_ref, xo_ref, tok_ref, lg_ref = rest
    else:
        xo_ref, tok_ref = rest
    gavg = gavg_ref[...]
    y_ret = _head_norm(yrf_ref[...] + yrb_ref[...], gavg, vec_ref[0:1, :], True) * _silu(gret_ref[...])
    u = u_ref[...]
    z = _gelu_tanh(ys5_ref[...] + vec_ref[3:4, :] * u)
    y_s5 = z * _sigmoid(_dot(z.astype(BF16), glu_ref[...]) + vec_ref[4:5, :])
    y_hg = _head_norm(yhf_ref[...] + yhb_ref[...], gavg, vec_ref[1:2, :], False) * _silu(ghg_ref[...])
    y_gla = _head_norm(ygf_ref[...] + ygb_ref[...], gavg, vec_ref[2:3, :], False) * _silu(ggl_ref[...])
    acc = _dot(y_ret.astype(BF16), wo_ref[0:256, :])
    acc = acc + _dot(y_s5.astype(BF16), wo_ref[256:512, :])
    acc = acc + _dot(y_hg.astype(BF16), wo_ref[512:768, :])
    acc = acc + _dot(y_gla.astype(BF16), wo_ref[768:1024, :])
    x_new = x_ref[...] + mod_ref[M_GATE1:M_GATE1 + 1, :] * acc
    xo_ref[...] = x_new
    tok = _rms(x_new, nw_ref[...]) * (1.0 + mod_ref[M_SCALE2:M_SCALE2 + 1, :]) + mod_ref[M_SHIFT2:M_SHIFT2 + 1, :]
    tok_ref[...] = tok
    if with_router:
        lg_ref[...] = _dot_hi(tok, rw_ref[...]) + rb_ref[...]


def _mix_out(xs, mods, p_ret, yrf, yrb, p_s5, ys5, p_hg, yhf, yhb, p_gla, ygf, ygb,
             vec, glu_w, w_out, norm2_w, router, n_ctx_tiles):
    bsz, t_all, d = xs.shape
    row, mod = _tile_specs(n_ctx_tiles)
    head = np.arange(MIX_W) // (MIX_W // N_HEADS)
    gavg = jnp.asarray((head[:, None] == head[None, :]) / (MIX_W // N_HEADS), BF16)
    ins = [xs, mods, p_ret, yrf, yrb, p_s5, ys5, p_hg, yhf, yhb, p_gla, ygf, ygb,
           vec, glu_w.astype(BF16), gavg, w_out.astype(BF16), norm2_w[None, :]]
    specs = [row(d), mod, row(256, 3), row(256), row(256), row(256), row(256), row(256, 4), row(256), row(256),
             row(256, 2), row(256), row(256), _full(vec.shape), _full(glu_w.shape), _full(gavg.shape),
             _full(w_out.shape), _full((1, d))]
    out_specs = [row(d), row(d)]
    out_shape = [jax.ShapeDtypeStruct((bsz, t_all, d), F32), jax.ShapeDtypeStruct((bsz, t_all, d), F32)]
    if router is not None:
        rw = jnp.pad(router[0], ((0, 0), (0, 128 - N_EXPERTS)))
        rb = jnp.pad(router[1], (0, 128 - N_EXPERTS))[None, :]
        ins += [rw, rb]
        specs += [_full(rw.shape), _full(rb.shape)]
        out_specs.append(row(128))
        out_shape.append(jax.ShapeDtypeStruct((bsz, t_all, 128), F32))
    return pl.pallas_call(
        functools.partial(_out_kernel, with_router=router is not None), grid=(bsz, t_all // ROW_TILE),
        in_specs=specs, out_specs=out_specs, out_shape=out_shape,
        compiler_params=_cparams(("arbitrary", "arbitrary")),
    )(*ins)


FF_SPLIT = 2


def _swiglu(tok16, w1_ref, w3_ref, w2_ref):
    d_ff = w1_ref.shape[-1]
    step = d_ff // FF_SPLIT
    acc = None
    for j in range(FF_SPLIT):
        sl = slice(j * step, (j + 1) * step)
        h = _silu(_dot(tok16, w1_ref[:, sl])) * _dot(tok16, w3_ref[:, sl])
        part = _dot(h.astype(BF16), w2_ref[sl, :])
        acc = part if acc is None else acc + part
    return acc


def _ffn_kernel(x_ref, tok_ref, mod_ref, w1_ref, w3_ref, w2_ref, o_ref):
    f = _swiglu(tok_ref[...].astype(BF16), w1_ref, w3_ref, w2_ref)
    o_ref[...] = x_ref[...] + mod_ref[M_GATE2:M_GATE2 + 1, :] * f


def _dense_ffn(xs, tok, mods, w1, w3, w2, n_ctx_tiles):
    bsz, t_all, d = xs.shape
    row, mod = _tile_specs(n_ctx_tiles)
    once = lambda a: pl.BlockSpec(a.shape, lambda b, t: (0, 0), pipeline_mode=pl.Buffered(1))
    w1, w3, w2 = w1.astype(BF16), w3.astype(BF16), w2.astype(BF16)
    return pl.pallas_call(
        _ffn_kernel, grid=(bsz, t_all // ROW_TILE),
        in_specs=[row(d), row(d), mod, once(w1), once(w3), once(w2)],
        out_specs=row(d), out_shape=jax.ShapeDtypeStruct(xs.shape, F32),
        compiler_params=_cparams(("arbitrary", "arbitrary")),
    )(xs, tok, mods, w1, w3, w2)


MOE_BLOCK = 512


def _routing(logits):
    n = logits.shape[0]
    top_v, top_i = lax.top_k(logits[:, :N_EXPERTS], TOP_K)
    gates = jax.nn.softmax(top_v, axis=-1)
    flat_e = top_i.reshape(-1).astype(jnp.int32)
    onehot = (flat_e[:, None] == jnp.arange(N_EXPERTS, dtype=jnp.int32)[None, :]).astype(jnp.int32)
    csum = jnp.cumsum(onehot, axis=0)
    counts = csum[-1]
    rank = jnp.take_along_axis(csum, flat_e[:, None], axis=1)[:, 0] - 1
    padded = (counts + MOE_BLOCK - 1) // MOE_BLOCK * MOE_BLOCK
    pends = jnp.cumsum(padded)
    dest = (pends - padded)[flat_e] + rank
    n_blocks = -(-(n * TOP_K) // MOE_BLOCK) + N_EXPERTS
    cap = n_blocks * MOE_BLOCK
    flat_tok = jnp.arange(n * TOP_K, dtype=jnp.int32) // TOP_K
    row_tok = jnp.zeros((cap,), jnp.int32).at[dest].set(flat_tok)
    block_start = jnp.arange(n_blocks, dtype=jnp.int32) * MOE_BLOCK
    block_e = jnp.minimum(jnp.searchsorted(pends, block_start, side='right'), N_EXPERTS - 1).astype(jnp.int32)
    return dest.reshape(n, TOP_K), gates, row_tok, block_e


def _row_copy(src_hbm, row, dst_ref, sem):
    return pltpu.make_async_copy(src_hbm.at[pl.ds(row, 1), :], dst_ref, sem)


def _rows_wait(src_hbm, dst_ref, sem):
    pltpu.make_async_copy(src_hbm.at[pl.ds(0, dst_ref.shape[0]), :], dst_ref, sem).wait()


def _moe_kernel(be_ref, idx_ref, idx_next_ref, tok_hbm, w1_ref, w3_ref, w2_ref, o_ref, xg_ref, sem):
    del be_ref
    i, j = pl.program_id(0), pl.program_id(1)
    slot = i % 2
    bm = xg_ref.shape[1]
    part_rows = bm // FF_SPLIT

    @pl.when((i == 0) & (j == 0))
    def _():
        def body(r, carry):
            _row_copy(tok_hbm, idx_ref[0, r], xg_ref.at[0, pl.ds(r, 1), :], sem.at[0]).start()
            return carry
        lax.fori_loop(0, bm, body, 0)

    @pl.when(j == 0)
    def _():
        _rows_wait(tok_hbm, xg_ref.at[slot], sem.at[slot])

    base = j * part_rows
    for r in range(part_rows):
        _row_copy(tok_hbm, idx_next_ref[0, base + r], xg_ref.at[1 - slot, pl.ds(base + r, 1), :],
                  sem.at[1 - slot]).start()

    x16 = xg_ref[slot].astype(BF16)
    h = _silu(_dot(x16, w1_ref[...])) * _dot(x16, w3_ref[...])
    part = _dot(h.astype(BF16), w2_ref[...])

    @pl.when(j == 0)
    def _():
        o_ref[...] = part

    @pl.when(j > 0)
    def _():
        o_ref[...] += part

    @pl.when((i == pl.num_programs(0) - 1) & (j == FF_SPLIT - 1))
    def _():
        _rows_wait(tok_hbm, xg_ref.at[1 - slot], sem.at[1 - slot])


def _moe_experts(tok, row_tok, block_e, w1, w3, w2):
    n, d = tok.shape
    bm = MOE_BLOCK
    n_blocks = block_e.shape[0]
    d_ff = w1.shape[-1]
    step = d_ff // FF_SPLIT
    idx = row_tok.reshape(n_blocks, 1, bm)
    smem_blk = lambda f: pl.BlockSpec((None, 1, bm), f, memory_space=pltpu.SMEM)
    grid_spec = pltpu.PrefetchScalarGridSpec(
        num_scalar_prefetch=1, grid=(n_blocks, FF_SPLIT),
        in_specs=[
            smem_blk(lambda i, j, be: (i, 0, 0)),
            smem_blk(lambda i, j, be: (jnp.minimum(i + 1, n_blocks - 1), 0, 0)),
            pl.BlockSpec(memory_space=pl.ANY),
            pl.BlockSpec((None, d, step), lambda i, j, be: (be[i], 0, j)),
            pl.BlockSpec((None, d, step), lambda i, j, be: (be[i], 0, j)),
            pl.BlockSpec((None, step, d), lambda i, j, be: (be[i], j, 0)),
        ],
        out_specs=pl.BlockSpec((bm, d), lambda i, j, be: (i, 0)),
        scratch_shapes=[pltpu.VMEM((2, bm, d), F32), pltpu.SemaphoreType.DMA((2,))],
    )
    return pl.pallas_call(
        _moe_kernel, grid_spec=grid_spec,
        out_shape=jax.ShapeDtypeStruct((n_blocks * bm, d), F32),
        compiler_params=_cparams(("arbitrary", "arbitrary")),
    )(block_e, idx, idx, tok, w1.astype(BF16), w3.astype(BF16), w2.astype(BF16))


def _combine_kernel(d_ref, dn_ref, x_ref, g_ref, mod_ref, y_hbm, o_ref, buf_ref, sem):
    i = pl.program_id(0)
    slot = i % 2
    tm = x_ref.shape[0]

    def gather(idx, to_slot):
        def body(r, carry):
            for k in range(TOP_K):
                _row_copy(y_hbm, idx[k, r], buf_ref.at[to_slot, pl.ds(k * tm + r, 1), :], sem.at[to_slot]).start()
            return carry
        lax.fori_loop(0, tm, body, 0, unroll=8)

    @pl.when(i == 0)
    def _():
        gather(d_ref, 0)

    @pl.when(i + 1 < pl.num_programs(0))
    def _():
        gather(dn_ref, 1 - slot)

    _rows_wait(y_hbm, buf_ref.at[slot], sem.at[slot])
    f = buf_ref[slot, 0:tm, :] * g_ref[:, 0:1] + buf_ref[slot, tm:2 * tm, :] * g_ref[:, 1:2]
    o_ref[...] = x_ref[...] + mod_ref[M_GATE2:M_GATE2 + 1, :] * f


def _moe_combine(xs, mods, ybuf, dest, gates, n_ctx_tiles):
    bsz, t_all, d = xs.shape
    tm = ROW_TILE
    tiles_b = t_all // tm
    n_tiles = bsz * tiles_b
    dt = dest.reshape(n_tiles, tm, TOP_K).transpose(0, 2, 1)
    smem_blk = lambda f: pl.BlockSpec((None, TOP_K, tm), f, memory_space=pltpu.SMEM)
    out = pl.pallas_call(
        _combine_kernel, grid=(n_tiles,),
        in_specs=[
            smem_blk(lambda i: (i, 0, 0)),
            smem_blk(lambda i: (jnp.minimum(i + 1, n_tiles - 1), 0, 0)),
            pl.BlockSpec((tm, d), lambda i: (i, 0)),
            pl.BlockSpec((tm, TOP_K), lambda i: (i, 0)),
            pl.BlockSpec((None, None, 8, d),
                         lambda i: (i // tiles_b, jnp.where(i % tiles_b >= n_ctx_tiles, 1, 0), 0, 0)),
            pl.BlockSpec(memory_space=pl.ANY),
        ],
        out_specs=pl.BlockSpec((tm, d), lambda i: (i, 0)),
        out_shape=jax.ShapeDtypeStruct((bsz * t_all, d), F32),
        scratch_shapes=[pltpu.VMEM((2, TOP_K * tm, d), F32), pltpu.SemaphoreType.DMA((2,))],
        compiler_params=_cparams(("arbitrary",)),
    )(dt, dt, xs.reshape(bsz * t_all, d), gates, mods, ybuf)
    return out.reshape(bsz, t_all, d)


def _final_kernel(x_ref, w_ref, o_ref):
    o_ref[...] = _rms(x_ref[...], w_ref[...])


def _final_norm(xs, w, n_ctx_tiles):
    bsz, t_all, d = xs.shape
    tm = ROW_TILE
    n_lat = t_all // tm - n_ctx_tiles
    return pl.pallas_call(
        _final_kernel, grid=(bsz, n_lat),
        in_specs=[pl.BlockSpec((None, tm, d), lambda b, t: (b, t + n_ctx_tiles, 0)), _full((1, d))],
        out_specs=pl.BlockSpec((None, tm, d), lambda b, t: (b, t, 0)),
        out_shape=jax.ShapeDtypeStruct((bsz, n_lat * tm, d), F32),
        compiler_params=_cparams(("arbitrary", "arbitrary")),
    )(xs, w[None, :])


def kernel(x, c, ctx, c_ctx, ada_w, ada_b, norm1_w, norm2_w, w_in, w_out, ret_gn_w, s5_lam_re, s5_lam_im, s5_log_dt, s5_b_re, s5_b_im, s5_c_re, s5_c_im, s5_d, s5_glu_w, s5_glu_b, hg_lb_logits, hg_norm_w, gla_wa2, gla_ba, gla_norm_w, ffn_w1, ffn_w3, ffn_w2, router_w, router_b, moe_w1, moe_w3, moe_w2, final_norm_w):
    bsz, seq, d = x.shape
    ctx_len = ctx.shape[1]
    depth = ada_w.shape[0]
    assert d == D_MODEL and ctx_len % ROW_TILE == 0 and seq % ROW_TILE == 0
    assert ROW_TILE == SCAN_CHUNK and seq % GRID_W == 0
    n_ctx_tiles = ctx_len // ROW_TILE

    xs = jnp.concatenate([ctx, x], axis=1)
    mods = _modulation(c, c_ctx, ada_w, ada_b)
    cos, sin = _rope_tables(seq, ctx_len)
    lb_sm = jax.nn.softmax(hg_lb_logits.astype(F32), axis=0)
    lower_bounds = jnp.clip(jnp.cumsum(lb_sm, axis=0) - lb_sm[0], 0.0, LB_CEIL)
    w_in_pad = jnp.pad(w_in, ((0, 0), (0, 0), (0, D_IN_PAD - D_IN))).astype(BF16)
    s5_names = (s5_lam_re, s5_lam_im, s5_log_dt, s5_b_re, s5_b_im, s5_c_re, s5_c_im)

    for l in range(depth):
        p_ret, p_s5, p_hg, p_gla = _project(xs, mods[l], norm1_w[l], w_in_pad[l], n_ctx_tiles)

        yrf = _ret_scan(p_ret, cos, sin, n_ctx_tiles, False)
        yrb = _ret_scan(p_ret, cos, sin, n_ctx_tiles, True)

        ops_f = _s5_operators(*(a[l, 0] for a in s5_names), reverse=False)
        ops_b = _s5_operators(*(a[l, 1] for a in s5_names), reverse=True)
        ys5 = _s5_mix(p_s5, ops_f, ops_b, ctx_len)

        lb = lower_bounds[l]
        lb_rows = jnp.zeros((8, MIX_W), F32).at[0].set(jnp.maximum(lb, LB_FLOOR)).at[1].set(1.0 - lb)
        yhf = _hg_scan(p_hg, lb_rows, n_ctx_tiles, False)
        yhb = _hg_scan(p_hg, lb_rows, n_ctx_tiles, True)

        ygs = []
        for dr in range(2):
            wa = jnp.zeros((128, 128), F32).at[GLA_RANK * dr:GLA_RANK * (dr + 1)].set(gla_wa2[l, dr])
            ygs.append(_gla_scan(p_gla, wa, gla_ba[l, dr][None, :], n_ctx_tiles, bool(dr)))

        vec = jnp.zeros((8, MIX_W), F32).at[0].set(ret_gn_w[l]).at[1].set(hg_norm_w[l]).at[2].set(gla_norm_w[l])
        vec = vec.at[3].set(s5_d[l]).at[4].set(s5_glu_b[l])
        j = l // 2
        router = (router_w[j], router_b[j]) if l % 2 == 1 else None
        outs = _mix_out(xs, mods[l], p_ret, yrf, yrb, p_s5, ys5, p_hg, yhf, yhb, p_gla, ygs[0], ygs[1],
                        vec, s5_glu_w[l], w_out[l], norm2_w[l], router, n_ctx_tiles)
        if l % 2 == 0:
            x_new, tok = outs
            xs = _dense_ffn(x_new, tok, mods[l], ffn_w1[j], ffn_w3[j], ffn_w2[j], n_ctx_tiles)
        else:
            x_new, tok, logits = outs
            n_tok = bsz * xs.shape[1]
            dest, gates, row_tok, block_e = _routing(logits.reshape(n_tok, 128))
            ybuf = _moe_experts(tok.reshape(n_tok, d), row_tok, block_e, moe_w1[j], moe_w3[j], moe_w2[j])
            xs = _moe_combine(x_new, mods[l], ybuf, dest, gates, n_ctx_tiles)

    return _final_norm(xs, final_norm_w, n_ctx_tiles)
```

```python
import functools
import math

import numpy as np
import jax
import jax.numpy as jnp
from jax import lax
from jax.experimental import pallas as pl
from jax.experimental.pallas import tpu as pltpu

F32 = jnp.float32
BF16 = jnp.bfloat16

D_MODEL = 1024
GRID_W = 64
MIX_W = 256
NORM_EPS = 1e-6
ROPE_BASE = 10000.0
LB_FLOOR = 1e-30
LB_CEIL = 1.0 - 1e-6
N_HEADS = 4
S5_GROUP = 16
S5_GROUPS = 16
S5_STATE = 64
S5_SUB = 16
GLA_RANK = 16
GLA_TAU = 16.0
N_EXPERTS = 8
TOP_K = 2
D_IN = 3360
D_IN_PAD = 3456

SCAN_CHUNK = 256
SCAN_SUB = 16
TAME_LIMIT = 40.0
NEG_BIG = -1e30
VMEM_LIMIT = 56 * 1024 * 1024


def _cparams(sem):
    return pltpu.CompilerParams(dimension_semantics=sem, vmem_limit_bytes=VMEM_LIMIT)


def _full(shape):
    n = len(shape)
    return pl.BlockSpec(shape, lambda *_: (0,) * n)


def _dot(a, b):
    return jnp.dot(a, b, preferred_element_type=F32)


def _dot_nt(a, b):
    return lax.dot_general(a, b, (((1,), (1,)), ((), ())), preferred_element_type=F32)


def _dot_hi(a, b):
    return jnp.dot(a, b, preferred_element_type=F32, precision=lax.Precision.HIGHEST)


def _sigmoid(z):
    return 0.5 * (jnp.tanh(0.5 * z) + 1.0)


def _silu(z):
    return z * _sigmoid(z)


def _log_sigmoid(z):
    return jnp.minimum(z, 0.0) - jnp.log(1.0 + jnp.exp(-jnp.abs(z)))


def _scan_constants(n_heads, wk, wv, reverse):
    c, s, hc = SCAN_CHUNK, SCAN_SUB, SCAN_CHUNK // 2
    i = np.arange(c)[:, None]
    r = np.arange(c)[None, :]
    cum = (r >= i) if reverse else (r <= i)
    ih = np.arange(hc)[:, None]
    jh = np.arange(hc)[None, :]
    causal = (jh >= ih) if reverse else (jh <= ih)
    masks = [((ih // s) == (jh // s)) & causal]
    m = s
    while m < hc:
        same = (ih // (2 * m)) == (jh // (2 * m))
        i_hi, j_hi = (ih % (2 * m)) >= m, (jh % (2 * m)) >= m
        masks.append(same & (~i_hi & j_hi if reverse else i_hi & ~j_hi))
        m *= 2
    dk, dv = wk // n_heads, wv // n_heads
    lk, lv = np.arange(wk), np.arange(wv)
    hmv = (lv[None, :] // dv) == np.arange(n_heads)[:, None]
    bd = (lv[:, None] // dv) == (lk[None, :] // dk)
    return dict(
        cm=jnp.asarray(np.concatenate([cum] * 2, axis=1), BF16),
        pm=jnp.asarray(np.stack([np.tile(x, (1, n_heads)) for x in masks]), F32),
        hmv=jnp.asarray(hmv[:, None, :], BF16),
        bd=jnp.asarray(bd, F32),
        gmat=jnp.asarray(bd.T, BF16),
    )


def _split3(g):
    hi = g.astype(BF16)
    r1 = g - hi.astype(F32)
    mid = r1.astype(BF16)
    lo = (r1 - mid.astype(F32)).astype(BF16)
    return hi, mid, lo


def _head_scores(qd, k16, n_heads):
    r, wk = qd.shape
    dk = wk // n_heads
    per_col = 128 // dk
    lane_head = lax.broadcasted_iota(jnp.int32, (1, 128), 1) // dk
    blocks = []
    for col in range(wk // 128):
        qc = qd[:, col * 128:(col + 1) * 128]
        lhs = jnp.concatenate([jnp.where(lane_head == h, qc, 0.0) for h in range(per_col)], axis=0)
        sc = _dot_nt(lhs.astype(BF16), k16[:, col * 128:(col + 1) * 128])
        blocks += [sc[h * r:(h + 1) * r, :] for h in range(per_col)]
    return jnp.concatenate(blocks, axis=1)


def _ref_rows(bb_ref, rows, span):
    wk = bb_ref.shape[1]
    pieces = [jnp.zeros((span, wk), F32) if r is None else jnp.broadcast_to(bb_ref[r:r + 1, :], (span, wk))
              for r in rows]
    return pieces[0] if len(pieces) == 1 else jnp.concatenate(pieces, axis=0)


def _scan_core(q, k, v, g, cm_ref, pm_ref, hmv_ref, bd_ref, st_ref, bb_ref, o_ref, *, reverse):
    c, s, hc = SCAN_CHUNK, SCAN_SUB, SCAN_CHUNK // 2
    wk, wv = q.shape[1], v.shape[1]
    n_heads = hmv_ref.shape[0]
    g_hi = g.astype(BF16)
    g_lo = (g - g_hi.astype(F32)).astype(BF16)
    b = _dot(cm_ref[...], jnp.concatenate([g_hi, g_lo], axis=0))
    bb_ref[...] = b
    end = 0 if reverse else c - 1
    b_end = bb_ref[end:end + 1, :]
    st = st_ref[...]
    o_inter = _dot_nt((q * jnp.exp(b)).astype(BF16), st.astype(BF16))
    upd = _dot(v.T.astype(BF16), (k * jnp.exp(b_end - b)).astype(BF16))
    st_ref[...] = st * jnp.exp(b_end) + upd * bd_ref[...]

    v16 = v.astype(BF16)
    vexp = [_expand_heads(v16[h0:h0 + hc], hmv_ref[...]) for h0 in (0, hc)]
    halves = (slice(0, hc), slice(hc, c))

    n_blk = c // s
    if reverse:
        rows = [(i + 1) * s for i in range(n_blk - 1)] + [None]
    else:
        rows = [None] + [i * s - 1 for i in range(1, n_blk)]
    dq = b - _ref_rows(bb_ref, rows, s)
    low = jnp.min(jnp.min(dq, axis=1, keepdims=True), axis=0, keepdims=True)
    qd = q * jnp.exp(dq)
    k16 = (k * jnp.exp(jnp.minimum(-dq, TAME_LIMIT))).astype(BF16)
    pm_diag = pm_ref[0] * (low >= -TAME_LIMIT).astype(F32)
    a = [_head_scores(qd[h], k16[h], n_heads) * pm_diag for h in halves]

    m, lev = s, 1
    while m < hc:
        n_pair = c // (2 * m)
        rows = [p * 2 * m + (m if reverse else m - 1) for p in range(n_pair)]
        ed = jnp.exp(-jnp.abs(b - _ref_rows(bb_ref, rows, 2 * m)))
        qd, k16 = q * ed, (k * ed).astype(BF16)
        a = [a[i] + _head_scores(qd[h], k16[h], n_heads) * pm_ref[lev] for i, h in enumerate(halves)]
        m, lev = 2 * m, lev + 1

    ed = jnp.exp(-jnp.abs(b - _ref_rows(bb_ref, [hc if reverse else hc - 1], c)))
    qd, k16 = q * ed, (k * ed).astype(BF16)
    late, early = (0, 1) if reverse else (1, 0)
    top = _head_scores(qd[halves[late]], k16[halves[early]], n_heads)

    o_ref[halves[early], :] = o_inter[halves[early]] + _dot(a[early].astype(BF16), vexp[early])
    a_late = jnp.concatenate([a[late], top], axis=1).astype(BF16)
    o_ref[halves[late], :] = o_inter[halves[late]] + _dot(a_late, jnp.concatenate([vexp[late], vexp[early]], axis=0))

    return low[0, 0] < -TAME_LIMIT


def _explicit_diagonal(q, k, v, gmat_ref, bb_ref, qb_ref, kb_ref, vb_ref, wj_ref, o_ref, reverse):
    c, s = SCAN_CHUNK, SCAN_SUB
    wv = v.shape[1]
    qb_ref[...] = q
    kb_ref[...] = k
    vb_ref[...] = v
    rows = lax.broadcasted_iota(jnp.int32, (s, 1), 0)

    def blk(ib, carry):
        base = pl.multiple_of(ib * s, s)
        bblk = bb_ref[pl.ds(base, s), :]
        qblk = qb_ref[pl.ds(base, s), :]
        for jo in range(s):
            bj = bb_ref[pl.ds(base + jo, 1), :]
            kj = kb_ref[pl.ds(base + jo, 1), :]
            valid = (rows <= jo) if reverse else (rows >= jo)
            w = qblk * kj * jnp.exp(jnp.where(valid, bblk - bj, NEG_BIG))
            wj_ref[pl.ds(jo * s, s), :] = w.astype(BF16)
        r = _dot(wj_ref[...], gmat_ref[...])
        acc = jnp.zeros((s, wv), F32)
        for jo in range(s):
            acc = acc + r[jo * s:(jo + 1) * s, :] * vb_ref[pl.ds(base + jo, 1), :]
        o_ref[pl.ds(base, s), :] += acc
        return carry

    lax.fori_loop(0, c // s, blk, 0)


def _rope(t, cos, sin_signed):
    outs = []
    for h in range(t.shape[1] // 128):
        th = t[:, h * 128:(h + 1) * 128]
        lane = lax.broadcasted_iota(jnp.int32, th.shape, 1)
        partner = jnp.where((lane % 32) < 16, pltpu.roll(th, 128 - 16, 1), pltpu.roll(th, 16, 1))
        outs.append(th * cos[:, h * 128:(h + 1) * 128] + partner * sin_signed[:, h * 128:(h + 1) * 128])
    return jnp.concatenate(outs, axis=1)


def _expand_heads(x16, hm):
    return (x16[None] * hm).reshape(hm.shape[0] * x16.shape[0], x16.shape[1])


def _ret_constants(reverse):
    c, hc = SCAN_CHUNK, SCAN_CHUNK // 2
    lg = jnp.log1p(-(2.0 ** (-5.0 - jnp.arange(N_HEADS, dtype=F32))))
    lg = lg[::-1] if reverse else lg
    lane = jnp.repeat(lg, MIX_W // N_HEADS)[None, :]
    i = jnp.arange(c, dtype=F32)[:, None]
    steps_in = (c - i) if reverse else (i + 1.0)
    steps_out = i if reverse else (c - 1.0 - i)
    dec = jnp.concatenate([jnp.exp(lane * steps_in), jnp.exp(lane * steps_out),
                           jnp.broadcast_to(jnp.exp(lane * c), (8, MIX_W))], axis=0)
    ih = jnp.arange(hc, dtype=F32)[:, None]
    jh = jnp.arange(hc, dtype=F32)[None, :]
    dist = (jh - ih) if reverse else (ih - jh)
    per_head = lambda d: jnp.concatenate(
        [jnp.where(d >= 0, jnp.exp(lg[h] * jnp.maximum(d, 0.0)), 0.0) for h in range(N_HEADS)], axis=1)
    dm = jnp.stack([per_head(dist), per_head(dist + hc)])
    hd = np.arange(MIX_W) // (MIX_W // N_HEADS)
    hm = jnp.asarray((hd[None, :] == np.arange(N_HEADS)[:, None])[:, None, :], BF16)
    bd = jnp.asarray(hd[:, None] == hd[None, :], F32)
    return dec, dm, hm, bd


SCAN_BATCH = 2


def _ret_kernel(p_ref, cos_ref, sin_ref, dec_ref, dm_ref, hm_ref, bd_ref, o_ref, st_ref, *, reverse):
    @pl.when(pl.program_id(1) == 0)
    def _():
        st_ref[...] = jnp.zeros_like(st_ref)

    c, hc = SCAN_CHUNK, SCAN_CHUNK // 2
    cos, sin = cos_ref[...], sin_ref[...]
    hm = hm_ref[...]
    halves = (slice(0, hc), slice(hc, c))
    late, early = (0, 1) if reverse else (1, 0)
    for bi in range(p_ref.shape[0]):
        q = _rope(p_ref[bi, :, 0:256], cos, sin)
        k = _rope(p_ref[bi, :, 256:512], cos, sin) * (64.0 ** -0.5)
        v = p_ref[bi, :, 512:768]
        st = st_ref[bi]
        o_inter = _dot_nt((q * dec_ref[0:c, :]).astype(BF16), st.astype(BF16))
        upd = _dot(v.T.astype(BF16), (k * dec_ref[c:2 * c, :]).astype(BF16))
        st_ref[bi] = st * dec_ref[2 * c:2 * c + 1, :] + upd * bd_ref[...]

        k16, v16 = k.astype(BF16), v.astype(BF16)
        vx = [_expand_heads(v16[h], hm) for h in halves]
        a_early = _head_scores(q[halves[early]], k16[halves[early]], N_HEADS) * dm_ref[0]
        a_late = _head_scores(q[halves[late]], k16[halves[late]], N_HEADS) * dm_ref[0]
        top = _head_scores(q[halves[late]], k16[halves[early]], N_HEADS) * dm_ref[1]
        o_ref[bi, halves[early], :] = o_inter[halves[early]] + _dot(a_early.astype(BF16), vx[early])
        a_late = jnp.concatenate([a_late, top], axis=1).astype(BF16)
        o_ref[bi, halves[late], :] = (o_inter[halves[late]]
                                      + _dot(a_late, jnp.concatenate([vx[late], vx[early]], axis=0)))


def _hg_inputs(p_ref, extra, reverse):
    (lb_ref,) = extra
    q = p_ref[:, 0:256]
    z = p_ref[:, 512:768] if reverse else p_ref[:, 256:512]
    v = p_ref[:, 768:1024]
    lb_floor, one_m_lb = lb_ref[0:1, :], lb_ref[1:2, :]
    sg = _sigmoid(z)
    return q, one_m_lb * (1.0 - sg), v, jnp.log(lb_floor + one_m_lb * sg)


def _gla_inputs(p_ref, extra, reverse):
    wa_ref, ba_ref = extra
    q = p_ref[:, 0:128]
    k = p_ref[:, 128:256] * (32.0 ** -0.5)
    v = p_ref[:, 256:512]
    g = _log_sigmoid(_dot_hi(p_ref[:, 768:896], wa_ref[...]) + ba_ref[...]) * (1.0 / GLA_TAU)
    return q, k, v, g


def _scan_kernel(p_ref, *rest, inputs_fn, n_extra, reverse):
    extra, rest = rest[:n_extra], rest[n_extra:]
    (cm_ref, pm_ref, hmv_ref, bd_ref, gmat_ref, o_ref, st_ref, bb_ref, *fallback_scratch) = rest

    @pl.when(pl.program_id(1) == 0)
    def _():
        st_ref[...] = jnp.zeros_like(st_ref)

    n_b = p_ref.shape[0]
    untame = []
    for bi in range(n_b):
        q, k, v, g = inputs_fn(p_ref.at[bi], extra, reverse)
        untame.append(_scan_core(q, k, v, g, cm_ref, pm_ref, hmv_ref, bd_ref,
                                 st_ref.at[bi], bb_ref.at[bi], o_ref.at[bi], reverse=reverse))
    for bi in range(n_b):
        @pl.when(untame[bi])
        def _(bi=bi):
            q, k, v, _ = inputs_fn(p_ref.at[bi], extra, reverse)
            _explicit_diagonal(q, k, v, gmat_ref, bb_ref.at[bi], *fallback_scratch, o_ref.at[bi], reverse)


def _scan_tile_map(n_tiles, n_ctx_tiles, reverse):
    if not reverse:
        return lambda step: step
    return lambda step: jnp.where(step < n_ctx_tiles, n_ctx_tiles - 1 - step, n_tiles - 1 - (step - n_ctx_tiles))


def _scan_call(inputs_fn, p, extra, wk, wv, n_ctx_tiles, reverse):
    bsz, t_all, wp = p.shape
    c, s, nb = SCAN_CHUNK, SCAN_SUB, math.gcd(SCAN_BATCH, bsz)
    n_tiles = t_all // c
    cst = _scan_constants(N_HEADS, wk, wv, reverse)
    tile = _scan_tile_map(n_tiles, n_ctx_tiles, reverse)
    consts = [cst[n] for n in ('cm', 'pm', 'hmv', 'bd', 'gmat')]
    in_specs = [pl.BlockSpec((nb, c, wp), lambda b, t: (b, tile(t), 0))]
    in_specs += [_full(x.shape) for x in extra] + [_full(x.shape) for x in consts]
    return pl.pallas_call(
        functools.partial(_scan_kernel, inputs_fn=inputs_fn, n_extra=len(extra), reverse=reverse),
        grid=(bsz // nb, n_tiles),
        in_specs=in_specs,
        out_specs=pl.BlockSpec((nb, c, wv), lambda b, t: (b, tile(t), 0)),
        out_shape=jax.ShapeDtypeStruct((bsz, t_all, wv), F32),
        scratch_shapes=[
            pltpu.VMEM((nb, wv, wk), F32), pltpu.VMEM((nb, c, wk), F32),
            pltpu.VMEM((c, wk), F32), pltpu.VMEM((c, wk), F32), pltpu.VMEM((c, wv), F32),
            pltpu.VMEM((s * s, wk), BF16),
        ],
        compiler_params=_cparams(("arbitrary", "arbitrary")),
    )(p, *extra, *consts)


def _ret_scan(p_ret, cos, sin, n_ctx_tiles, reverse):
    bsz, t_all, wp = p_ret.shape
    c, nb = SCAN_CHUNK, math.gcd(SCAN_BATCH, bsz)
    tile = _scan_tile_map(t_all // c, n_ctx_tiles, reverse)
    consts = _ret_constants(reverse)
    rows = lambda w: pl.BlockSpec((c, w), lambda b, t: (tile(t), 0))
    return pl.pallas_call(
        functools.partial(_ret_kernel, reverse=reverse),
        grid=(bsz // nb, t_all // c),
        in_specs=[pl.BlockSpec((nb, c, wp), lambda b, t: (b, tile(t), 0)), rows(MIX_W), rows(MIX_W)]
        + [_full(x.shape) for x in consts],
        out_specs=pl.BlockSpec((nb, c, MIX_W), lambda b, t: (b, tile(t), 0)),
        out_shape=jax.ShapeDtypeStruct((bsz, t_all, MIX_W), F32),
        scratch_shapes=[pltpu.VMEM((nb, MIX_W, MIX_W), F32)],
        compiler_params=_cparams(("arbitrary", "arbitrary")),
    )(p_ret, cos, sin, *consts)


def _hg_scan(p_hg, lb_rows, n_ctx_tiles, reverse):
    return _scan_call(_hg_inputs, p_hg, [lb_rows], 256, 256, n_ctx_tiles, reverse)


def _gla_scan(p_gla, wa, ba, n_ctx_tiles, reverse):
    return _scan_call(_gla_inputs, p_gla, [wa, ba], 128, 256, n_ctx_tiles, reverse)


def _rope_tables(seq, ctx_len):
    t_idx = jnp.arange(seq, dtype=jnp.int32)
    row = (t_idx // GRID_W).astype(F32)
    col = (t_idx % GRID_W).astype(F32)
    freqs = ROPE_BASE ** (-jnp.arange(16, dtype=F32) / 16)
    ang_r = row[:, None] * freqs[None, :]
    ang_c = col[:, None] * freqs[None, :]
    cos = jnp.concatenate([jnp.cos(ang_r)] * 2 + [jnp.cos(ang_c)] * 2, axis=1)
    sin = jnp.concatenate([-jnp.sin(ang_r), jnp.sin(ang_r), -jnp.sin(ang_c), jnp.sin(ang_c)], axis=1)
    cos = jnp.concatenate([jnp.ones((ctx_len, 64), F32), cos], axis=0)
    sin = jnp.concatenate([jnp.zeros((ctx_len, 64), F32), sin], axis=0)
    return jnp.tile(cos, (1, N_HEADS)), jnp.tile(sin, (1, N_HEADS))


def _s5_operators(lam_re, lam_im, log_dt, b_re, b_im, c_re, c_im):
    n = S5_SUB
    lam = lax.complex(lam_re, lam_im)
    log_a = lam * jnp.exp(log_dt)[..., None]
    a_bar = jnp.exp(log_a)
    b_bar = ((a_bar - 1) / lam)[..., None] * lax.complex(b_re, b_im)
    c_mat = lax.complex(c_re, c_im)
    step = np.arange(n)
    pw = jnp.exp(log_a[..., None, :] * jnp.arange(n + 1, dtype=F32)[:, None])
    kern = jnp.einsum('...cp,...tp,...pd->...tcd', c_mat, pw[..., :n, :], b_bar).real
    lag = step[None, :] - step[:, None]
    m = jnp.where((lag >= 0)[:, :, None, None], kern[..., np.maximum(lag, 0), :, :], 0.0)
    m = jnp.moveaxis(m, -1, -3)
    qc = pw[..., n - 1 - step, :][..., None] * b_bar[..., None, :, :]
    qc = jnp.swapaxes(qc, -1, -2)
    z = c_mat[..., None, :, :] * pw[..., step + 1, :][..., None, :]
    z = jnp.moveaxis(z, -1, -3)
    m = jnp.stack([m[:, 0], jnp.flip(m[:, 1], axis=(-4, -2))], axis=1)
    qc = jnp.stack([qc[:, 0], jnp.flip(qc[:, 1], axis=-3)], axis=1)
    z = jnp.stack([z[:, 0], jnp.flip(z[:, 1], axis=-2)], axis=1)
    lead = m.shape[:3]
    m = m.reshape(*lead, n * S5_GROUP, n * S5_GROUP)
    qc = qc.reshape(*lead, n * S5_GROUP, S5_STATE)
    z = z.reshape(*lead, S5_STATE, n * S5_GROUP)
    q = jnp.concatenate([qc.real, qc.imag, qc.imag, qc.real], axis=-1)
    pm = jnp.concatenate([z.real, -z.imag, jnp.zeros_like(z.real), jnp.zeros_like(z.real)], axis=-2)
    a_row = pw[..., n, :]
    a1 = jnp.concatenate([a_row.real] * 4, axis=-1)
    a2 = jnp.concatenate([-a_row.imag, a_row.imag, a_row.imag, -a_row.imag], axis=-1)
    a = jnp.stack([a1[:, 0], a2[:, 0], a1[:, 1], a2[:, 1]], axis=1)
    return ((m[:, 0] + m[:, 1]).astype(BF16), jnp.concatenate([q[:, 0], q[:, 1]], axis=-1).astype(BF16),
            pm[:, 0].astype(BF16), pm[:, 1].astype(BF16), a)


S5_GROUP_BLOCK = 8


S5_TILE = S5_SUB * S5_SUB


def _s5_in_kernel(u_ref, q_ref, wf_ref, wb_ref, uf_ref):
    gl = S5_GROUP_BLOCK * S5_GROUP

    def regroup(t, carry):
        x3 = u_ref[pl.ds(pl.multiple_of(t * S5_TILE, S5_TILE), S5_TILE), :].astype(BF16).reshape(S5_SUB, S5_SUB, gl)
        for k in range(S5_GROUP_BLOCK):
            piece = x3[:, :, k * S5_GROUP:(k + 1) * S5_GROUP].reshape(S5_SUB, S5_SUB * S5_GROUP)
            uf_ref[k, pl.ds(pl.multiple_of(t * S5_SUB, S5_SUB), S5_SUB), :] = piece
        return carry

    lax.fori_loop(0, u_ref.shape[0] // S5_TILE, regroup, 0)
    for k in range(S5_GROUP_BLOCK):
        w = _dot(uf_ref[k], q_ref[k])
        wf_ref[:, k, :] = w[:, :256]
        wb_ref[:, k, :] = w[:, 256:]


def _s5_state_kernel(wf_ref, wb_ref, a_ref, hf_ref, hb_ref, sf_ref, sb_ref):
    @pl.when(pl.program_id(0) == 0)
    def _():
        sf_ref[...] = jnp.zeros_like(sf_ref)
        sb_ref[...] = jnp.zeros_like(sb_ref)

    def swap(h):
        return jnp.concatenate([h[:, 128:], h[:, :128]], axis=1)

    n = wf_ref.shape[0]
    hf, hb = sf_ref[...], sb_ref[...]
    a1f, a2f, a1b, a2b = a_ref[0], a_ref[1], a_ref[2], a_ref[3]
    for i in range(n):
        hf_ref[i] = hf
        hf = a1f * hf + a2f * swap(hf) + wf_ref[i]
        j = n - 1 - i
        hb_ref[j] = hb
        hb = a1b * hb + a2b * swap(hb) + wb_ref[j]
    sf_ref[...] = hf
    sb_ref[...] = hb


def _s5_out_kernel(u_ref, m_ref, hf_ref, hb_ref, pf_ref, pb_ref, y_ref, ys_ref):
    for k in range(S5_GROUP_BLOCK):
        y = _dot(u_ref[k], m_ref[k])
        y = y + _dot(hf_ref[:, k, :].astype(BF16), pf_ref[k])
        y = y + _dot(hb_ref[:, k, :].astype(BF16), pb_ref[k])
        ys_ref[k] = y

    def regroup(t, carry):
        rows = pl.ds(pl.multiple_of(t * S5_SUB, S5_SUB), S5_SUB)
        pieces = [ys_ref[k, rows, :].reshape(S5_SUB, S5_SUB, S5_GROUP) for k in range(S5_GROUP_BLOCK)]
        y_ref[pl.ds(pl.multiple_of(t * S5_TILE, S5_TILE), S5_TILE), :] = (
            jnp.concatenate(pieces, axis=-1).reshape(S5_TILE, S5_GROUP_BLOCK * S5_GROUP))
        return carry

    lax.fori_loop(0, y_ref.shape[0] // S5_TILE, regroup, 0)


def _s5_mix(u, m, q, pf, pb, a_rows, ctx_len):
    bsz, t_all, _ = u.shape
    n, g = S5_SUB, S5_GROUPS
    n_rows = t_all // n
    blk = 16
    n_steps = n_rows // blk
    n_ctx = (ctx_len // n) // blk
    gb = S5_GROUP_BLOCK
    n_gb = g // gb
    nat_spec = pl.BlockSpec((None, t_all, gb * S5_GROUP), lambda b, gi: (b, 0, gi))
    u_spec = pl.BlockSpec((None, gb, n_rows, 256), lambda b, gi: (b, gi, 0, 0))
    st_spec = pl.BlockSpec((n_rows, gb, 256), lambda b, gi: (0, b * n_gb + gi, 0))
    hshape = jax.ShapeDtypeStruct((n_rows, bsz * g, 256), F32)
    wf, wb, uf = pl.pallas_call(
        _s5_in_kernel, grid=(bsz, n_gb),
        in_specs=[nat_spec, pl.BlockSpec((gb, 256, 512), lambda b, gi: (gi, 0, 0))],
        out_specs=[st_spec, st_spec, u_spec],
        out_shape=[hshape, hshape, jax.ShapeDtypeStruct((bsz, g, n_rows, 256), BF16)],
        compiler_params=_cparams(("arbitrary", "arbitrary")),
    )(u, q)

    a = jnp.tile(a_rows, (1, bsz, 1))
    fwd = lambda s: (s, 0, 0)
    bwd = lambda s: (jnp.where(s < n_ctx, n_ctx - 1 - s, n_steps - 1 - (s - n_ctx)), 0, 0)
    st_block = (blk, bsz * g, 256)
    hf, hb = pl.pallas_call(
        _s5_state_kernel, grid=(n_steps,),
        in_specs=[pl.BlockSpec(st_block, fwd), pl.BlockSpec(st_block, bwd), _full(a.shape)],
        out_specs=[pl.BlockSpec(st_block, fwd), pl.BlockSpec(st_block, bwd)],
        out_shape=[hshape, hshape],
        scratch_shapes=[pltpu.VMEM((bsz * g, 256), F32), pltpu.VMEM((bsz * g, 256), F32)],
        compiler_params=_cparams(("arbitrary",)),
    )(wf, wb, a)

    mat_spec = pl.BlockSpec((gb, 256, 256), lambda b, gi: (gi, 0, 0))
    return pl.pallas_call(
        _s5_out_kernel, grid=(bsz, n_gb),
        in_specs=[u_spec, mat_spec, st_spec, st_spec, mat_spec, mat_spec],
        out_specs=nat_spec,
        out_shape=jax.ShapeDtypeStruct((bsz, t_all, g * S5_GROUP), F32),
        scratch_shapes=[pltpu.VMEM((gb, n_rows, 256), F32)],
        compiler_params=_cparams(("arbitrary", "arbitrary")),
    )(uf, m, hf, hb, pf, pb)


ROW_TILE = 256
M_SHIFT1, M_SCALE1, M_GATE1, M_SHIFT2, M_SCALE2, M_GATE2 = range(6)


def _ada_kernel(cond_ref, w_ref, b_ref, o_ref):
    o_ref[...] = _dot_hi(_silu(cond_ref[...]), w_ref[...]) + b_ref[...]


def _modulation(c, c_ctx, ada_w, ada_b):
    depth, d, d6 = ada_w.shape
    bsz = c.shape[0]
    cond = jnp.zeros((8, d), F32).at[:bsz].set(c).at[bsz].set(c_ctx)
    out = pl.pallas_call(
        _ada_kernel, grid=(depth, d6 // d),
        in_specs=[_full((8, d)),
                  pl.BlockSpec((None, d, d), lambda l, j: (l, 0, j)),
                  pl.BlockSpec((None, 1, d), lambda l, j: (l, 0, j))],
        out_specs=pl.BlockSpec((None, 8, d), lambda l, j: (l, 0, j)),
        out_shape=jax.ShapeDtypeStruct((depth, 8, d6), F32),
        compiler_params=_cparams(("arbitrary", "arbitrary")),
    )(cond, ada_w, ada_b[:, None, :])
    out = out.reshape(depth, 8, 6, d)
    lat = out[:, :bsz]
    ctx = jnp.broadcast_to(out[:, bsz][:, None], lat.shape)
    mods = jnp.stack([ctx, lat], axis=2)
    return jnp.pad(mods, ((0, 0), (0, 0), (0, 0), (0, 2), (0, 0)))


def _rms(x, w):
    return x * lax.rsqrt(jnp.mean(x * x, axis=-1, keepdims=True) + NORM_EPS) * w


def _proj_kernel(x_ref, mod_ref, nw_ref, w_ref, ret_ref, s5_ref, hg_ref, gla_ref):
    h = _rms(x_ref[...], nw_ref[...]) * (1.0 + mod_ref[M_SCALE1:M_SCALE1 + 1, :]) + mod_ref[M_SHIFT1:M_SHIFT1 + 1, :]
    p = _dot(h.astype(BF16), w_ref[...])
    ret_ref[...] = p[:, 0:1024]
    s5_ref[...] = p[:, 1024:1280]
    hg_ref[...] = p[:, 1280:2560]
    gla_ref[...] = p[:, 2560:3456]


def _tile_specs(n_ctx_tiles):
    tm = ROW_TILE
    row = lambda w, j=0: pl.BlockSpec((None, tm, w), lambda b, t: (b, t, j))
    mod = pl.BlockSpec((None, None, 8, D_MODEL), lambda b, t: (b, jnp.where(t >= n_ctx_tiles, 1, 0), 0, 0))
    return row, mod


def _project(xs, mods, norm_w, w_pad, n_ctx_tiles):
    bsz, t_all, d = xs.shape
    row, mod = _tile_specs(n_ctx_tiles)
    widths = (1024, 256, 1280, 896)
    return pl.pallas_call(
        _proj_kernel, grid=(bsz, t_all // ROW_TILE),
        in_specs=[row(d), mod, _full((1, d)), _full(w_pad.shape)],
        out_specs=[row(w) for w in widths],
        out_shape=[jax.ShapeDtypeStruct((bsz, t_all, w), F32) for w in widths],
        compiler_params=_cparams(("arbitrary", "arbitrary")),
    )(xs, mods, norm_w[None, :], w_pad)


def _dot3(x, mat):
    hi, mid, lo = _split3(x)
    return _dot(hi, mat) + _dot(mid, mat) + _dot(lo, mat)


def _head_norm(y, gavg, w, center):
    if center:
        y = y - _dot3(y, gavg)
    return y * lax.rsqrt(_dot3(y * y, gavg) + NORM_EPS) * w


def _gelu_tanh(x):
    return 0.5 * x * (1.0 + jnp.tanh(math.sqrt(2.0 / math.pi) * (x + 0.044715 * (x * x * x))))


def _out_kernel(x_ref, mod_ref, gret_ref, yrf_ref, yrb_ref, u_ref, ys5_ref, ghg_ref, yhf_ref, yhb_ref,
                ggl_ref, ygf_ref, ygb_ref, vec_ref, glu_ref, gavg_ref, wo_ref, nw_ref, *rest, with_router):
    if with_router:
        rw_ref, rb_ref, xo_ref, tok_ref, lg_ref = rest
    else:
        xo_ref, tok_ref = rest
    gavg = gavg_ref[...]
    y_ret = _head_norm(yrf_ref[...] + yrb_ref[...], gavg, vec_ref[0:1, :], True) * _silu(gret_ref[...])
    u = u_ref[...]
    z = _gelu_tanh(ys5_ref[...] + vec_ref[3:4, :] * u)
    y_s5 = z * _sigmoid(_dot(z.astype(BF16), glu_ref[...]) + vec_ref[4:5, :])
    y_hg = _head_norm(yhf_ref[...] + yhb_ref[...], gavg, vec_ref[1:2, :], False) * _silu(ghg_ref[...])
    y_gla = _head_norm(ygf_ref[...] + ygb_ref[...], gavg, vec_ref[2:3, :], False) * _silu(ggl_ref[...])
    acc = _dot(y_ret.astype(BF16), wo_ref[0:256, :])
    acc = acc + _dot(y_s5.astype(BF16), wo_ref[256:512, :])
    acc = acc + _dot(y_hg.astype(BF16), wo_ref[512:768, :])
    acc = acc + _dot(y_gla.astype(BF16), wo_ref[768:1024, :])
    x_new = x_ref[...] + mod_ref[M_GATE1:M_GATE1 + 1, :] * acc
    xo_ref[...] = x_new
    tok = _rms(x_new, nw_ref[...]) * (1.0 + mod_ref[M_SCALE2:M_SCALE2 + 1, :]) + mod_ref[M_SHIFT2:M_SHIFT2 + 1, :]
    tok_ref[...] = tok
    if with_router:
        t_hi = tok.astype(BF16)
        t_lo = (tok - t_hi.astype(F32)).astype(BF16)
        lg_ref[...] = (_dot(t_hi, rw_ref[0]) + _dot(t_lo, rw_ref[0]) + _dot(t_hi, rw_ref[1])) + rb_ref[...]


def _mix_out(xs, mods, p_ret, yrf, yrb, p_s5, ys5, p_hg, yhf, yhb, p_gla, ygf, ygb,
             vec, glu_w, w_out, norm2_w, router, n_ctx_tiles):
    bsz, t_all, d = xs.shape
    row, mod = _tile_specs(n_ctx_tiles)
    head = np.arange(MIX_W) // (MIX_W // N_HEADS)
    gavg = jnp.asarray((head[:, None] == head[None, :]) / (MIX_W // N_HEADS), BF16)
    ins = [xs, mods, p_ret, yrf, yrb, p_s5, ys5, p_hg, yhf, yhb, p_gla, ygf, ygb,
           vec, glu_w.astype(BF16), gavg, w_out.astype(BF16), norm2_w[None, :]]
    specs = [row(d), mod, row(256, 3), row(256), row(256), row(256), row(256), row(256, 4), row(256), row(256),
             row(256, 2), row(256), row(256), _full(vec.shape), _full(glu_w.shape), _full(gavg.shape),
             _full(w_out.shape), _full((1, d))]
    out_specs = [row(d), row(d)]
    out_shape = [jax.ShapeDtypeStruct((bsz, t_all, d), F32), jax.ShapeDtypeStruct((bsz, t_all, d), F32)]
    if router is not None:
        rw = jnp.pad(router[0], ((0, 0), (0, 128 - N_EXPERTS)))
        rw_hi = rw.astype(BF16)
        rw = jnp.stack([rw_hi, (rw - rw_hi.astype(F32)).astype(BF16)])
        rb = jnp.pad(router[1], (0, 128 - N_EXPERTS))[None, :]
        ins += [rw, rb]
        specs += [_full(rw.shape), _full(rb.shape)]
        out_specs.append(row(128))
        out_shape.append(jax.ShapeDtypeStruct((bsz, t_all, 128), F32))
    return pl.pallas_call(
        functools.partial(_out_kernel, with_router=router is not None), grid=(bsz, t_all // ROW_TILE),
        in_specs=specs, out_specs=out_specs, out_shape=out_shape,
        compiler_params=_cparams(("arbitrary", "arbitrary")),
    )(*ins)


FF_SPLIT = 2


def _swiglu(tok16, w1_ref, w3_ref, w2_ref):
    d_ff = w1_ref.shape[-1]
    step = d_ff // FF_SPLIT
    acc = None
    for j in range(FF_SPLIT):
        sl = slice(j * step, (j + 1) * step)
        h = _silu(_dot(tok16, w1_ref[:, sl])) * _dot(tok16, w3_ref[:, sl])
        part = _dot(h.astype(BF16), w2_ref[sl, :])
        acc = part if acc is None else acc + part
    return acc


def _ffn_kernel(x_ref, tok_ref, mod_ref, w1_ref, w3_ref, w2_ref, o_ref):
    f = _swiglu(tok_ref[...].astype(BF16), w1_ref, w3_ref, w2_ref)
    o_ref[...] = x_ref[...] + mod_ref[M_GATE2:M_GATE2 + 1, :] * f


def _dense_ffn(xs, tok, mods, w1, w3, w2, n_ctx_tiles):
    bsz, t_all, d = xs.shape
    row, mod = _tile_specs(n_ctx_tiles)
    once = lambda a: pl.BlockSpec(a.shape, lambda b, t: (0, 0), pipeline_mode=pl.Buffered(1))
    w1, w3, w2 = w1.astype(BF16), w3.astype(BF16), w2.astype(BF16)
    return pl.pallas_call(
        _ffn_kernel, grid=(bsz, t_all // ROW_TILE),
        in_specs=[row(d), row(d), mod, once(w1), once(w3), once(w2)],
        out_specs=row(d), out_shape=jax.ShapeDtypeStruct(xs.shape, F32),
        compiler_params=_cparams(("arbitrary", "arbitrary")),
    )(xs, tok, mods, w1, w3, w2)


MOE_BLOCK = 512


def _routing(logits):
    n = logits.shape[0]
    top_v, top_i = lax.top_k(logits[:, :N_EXPERTS], TOP_K)
    gates = jax.nn.softmax(top_v, axis=-1)
    flat_e = top_i.reshape(-1).astype(jnp.int32)
    onehot = (flat_e[:, None] == jnp.arange(N_EXPERTS, dtype=jnp.int32)[None, :]).astype(jnp.int32)
    csum = jnp.cumsum(onehot, axis=0)
    counts = csum[-1]
    rank = jnp.take_along_axis(csum, flat_e[:, None], axis=1)[:, 0] - 1
    padded = (counts + MOE_BLOCK - 1) // MOE_BLOCK * MOE_BLOCK
    pends = jnp.cumsum(padded)
    dest = (pends - padded)[flat_e] + rank
    n_blocks = -(-(n * TOP_K) // MOE_BLOCK) + N_EXPERTS
    cap = n_blocks * MOE_BLOCK
    flat_tok = jnp.arange(n * TOP_K, dtype=jnp.int32) // TOP_K
    row_tok = jnp.zeros((cap,), jnp.int32).at[dest].set(flat_tok)
    block_start = jnp.arange(n_blocks, dtype=jnp.int32) * MOE_BLOCK
    block_e = jnp.minimum(jnp.searchsorted(pends, block_start, side='right'), N_EXPERTS - 1).astype(jnp.int32)
    return dest.reshape(n, TOP_K), gates, row_tok, block_e


def _row_copy(src_hbm, row, dst_ref, sem):
    return pltpu.make_async_copy(src_hbm.at[pl.ds(row, 1), :], dst_ref, sem)


def _rows_wait(src_hbm, dst_ref, sem):
    pltpu.make_async_copy(src_hbm.at[pl.ds(0, dst_ref.shape[0]), :], dst_ref, sem).wait()


def _moe_kernel(be_ref, idx_ref, idx_next_ref, tok_hbm, w1_ref, w3_ref, w2_ref, o_ref, xg_ref, sem):
    del be_ref
    i, j = pl.program_id(0), pl.program_id(1)
    slot = i % 2
    bm = xg_ref.shape[1]
    part_rows = bm // FF_SPLIT

    @pl.when((i == 0) & (j == 0))
    def _():
        def body(r, carry):
            _row_copy(tok_hbm, idx_ref[0, r], xg_ref.at[0, pl.ds(r, 1), :], sem.at[0]).start()
            return carry
        lax.fori_loop(0, bm, body, 0)

    @pl.when(j == 0)
    def _():
        _rows_wait(tok_hbm, xg_ref.at[slot], sem.at[slot])

    for jj in range(FF_SPLIT):
        @pl.when(j == jj)
        def _(jj=jj):
            for r in range(jj * part_rows, (jj + 1) * part_rows):
                _row_copy(tok_hbm, idx_next_ref[0, r], xg_ref.at[1 - slot, pl.ds(r, 1), :], sem.at[1 - slot]).start()

    x16 = xg_ref[slot].astype(BF16)
    h = _silu(_dot(x16, w1_ref[...])) * _dot(x16, w3_ref[...])
    part = _dot(h.astype(BF16), w2_ref[...])

    @pl.when(j == 0)
    def _():
        o_ref[...] = part

    @pl.when(j > 0)
    def _():
        o_ref[...] += part

    @pl.when((i == pl.num_programs(0) - 1) & (j == FF_SPLIT - 1))
    def _():
        _rows_wait(tok_hbm, xg_ref.at[1 - slot], sem.at[1 - slot])


def _moe_experts(tok, row_tok, block_e, w1, w3, w2):
    n, d = tok.shape
    bm = MOE_BLOCK
    n_blocks = block_e.shape[0]
    d_ff = w1.shape[-1]
    step = d_ff // FF_SPLIT
    idx = row_tok.reshape(n_blocks, 1, bm)
    smem_blk = lambda f: pl.BlockSpec((None, 1, bm), f, memory_space=pltpu.SMEM)
    grid_spec = pltpu.PrefetchScalarGridSpec(
        num_scalar_prefetch=1, grid=(n_blocks, FF_SPLIT),
        in_specs=[
            smem_blk(lambda i, j, be: (i, 0, 0)),
            smem_blk(lambda i, j, be: (jnp.minimum(i + 1, n_blocks - 1), 0, 0)),
            pl.BlockSpec(memory_space=pl.ANY),
            pl.BlockSpec((None, d, step), lambda i, j, be: (be[i], 0, j)),
            pl.BlockSpec((None, d, step), lambda i, j, be: (be[i], 0, j)),
            pl.BlockSpec((None, step, d), lambda i, j, be: (be[i], j, 0)),
        ],
        out_specs=pl.BlockSpec((bm, d), lambda i, j, be: (i, 0)),
        scratch_shapes=[pltpu.VMEM((2, bm, d), F32), pltpu.SemaphoreType.DMA((2,))],
    )
    return pl.pallas_call(
        _moe_kernel, grid_spec=grid_spec,
        out_shape=jax.ShapeDtypeStruct((n_blocks * bm, d), F32),
        compiler_params=_cparams(("arbitrary", "arbitrary")),
    )(block_e, idx, idx, tok, w1.astype(BF16), w3.astype(BF16), w2.astype(BF16))


def _combine_kernel(d_ref, dn_ref, x_ref, g_ref, mod_ref, y_hbm, o_ref, buf_ref, sem):
    i = pl.program_id(0)
    slot = i % 2
    tm = x_ref.shape[0]

    def gather(idx, to_slot):
        def body(r, carry):
            for k in range(TOP_K):
                _row_copy(y_hbm, idx[k, r], buf_ref.at[to_slot, pl.ds(k * tm + r, 1), :], sem.at[to_slot]).start()
            return carry
        lax.fori_loop(0, tm, body, 0, unroll=8)

    @pl.when(i == 0)
    def _():
        gather(d_ref, 0)

    @pl.when(i + 1 < pl.num_programs(0))
    def _():
        gather(dn_ref, 1 - slot)

    _rows_wait(y_hbm, buf_ref.at[slot], sem.at[slot])
    f = buf_ref[slot, 0:tm, :] * g_ref[:, 0:1] + buf_ref[slot, tm:2 * tm, :] * g_ref[:, 1:2]
    o_ref[...] = x_ref[...] + mod_ref[M_GATE2:M_GATE2 + 1, :] * f


def _moe_combine(xs, mods, ybuf, dest, gates, n_ctx_tiles):
    bsz, t_all, d = xs.shape
    tm = ROW_TILE
    tiles_b = t_all // tm
    n_tiles = bsz * tiles_b
    dt = dest.reshape(n_tiles, tm, TOP_K).transpose(0, 2, 1)
    smem_blk = lambda f: pl.BlockSpec((None, TOP_K, tm), f, memory_space=pltpu.SMEM)
    out = pl.pallas_call(
        _combine_kernel, grid=(n_tiles,),
        in_specs=[
            smem_blk(lambda i: (i, 0, 0)),
            smem_blk(lambda i: (jnp.minimum(i + 1, n_tiles - 1), 0, 0)),
            pl.BlockSpec((tm, d), lambda i: (i, 0)),
            pl.BlockSpec((tm, TOP_K), lambda i: (i, 0)),
            pl.BlockSpec((None, None, 8, d),
                         lambda i: (i // tiles_b, jnp.where(i % tiles_b >= n_ctx_tiles, 1, 0), 0, 0)),
            pl.BlockSpec(memory_space=pl.ANY),
        ],
        out_specs=pl.BlockSpec((tm, d), lambda i: (i, 0)),
        out_shape=jax.ShapeDtypeStruct((bsz * t_all, d), F32),
        scratch_shapes=[pltpu.VMEM((2, TOP_K * tm, d), F32), pltpu.SemaphoreType.DMA((2,))],
        compiler_params=_cparams(("arbitrary",)),
    )(dt, dt, xs.reshape(bsz * t_all, d), gates, mods, ybuf)
    return out.reshape(bsz, t_all, d)


def _final_kernel(x_ref, w_ref, o_ref):
    o_ref[...] = _rms(x_ref[...], w_ref[...])


def _final_norm(xs, w, n_ctx_tiles):
    bsz, t_all, d = xs.shape
    tm = ROW_TILE
    n_lat = t_all // tm - n_ctx_tiles
    return pl.pallas_call(
        _final_kernel, grid=(bsz, n_lat),
        in_specs=[pl.BlockSpec((None, tm, d), lambda b, t: (b, t + n_ctx_tiles, 0)), _full((1, d))],
        out_specs=pl.BlockSpec((None, tm, d), lambda b, t: (b, t, 0)),
        out_shape=jax.ShapeDtypeStruct((bsz, n_lat * tm, d), F32),
        compiler_params=_cparams(("arbitrary", "arbitrary")),
    )(xs, w[None, :])


def kernel(x, c, ctx, c_ctx, ada_w, ada_b, norm1_w, norm2_w, w_in, w_out, ret_gn_w, s5_lam_re, s5_lam_im, s5_log_dt, s5_b_re, s5_b_im, s5_c_re, s5_c_im, s5_d, s5_glu_w, s5_glu_b, hg_lb_logits, hg_norm_w, gla_wa2, gla_ba, gla_norm_w, ffn_w1, ffn_w3, ffn_w2, router_w, router_b, moe_w1, moe_w3, moe_w2, final_norm_w):
    bsz, seq, d = x.shape
    ctx_len = ctx.shape[1]
    depth = ada_w.shape[0]
    assert d == D_MODEL and ctx_len % ROW_TILE == 0 and seq % ROW_TILE == 0
    assert ROW_TILE == SCAN_CHUNK and seq % GRID_W == 0
    n_ctx_tiles = ctx_len // ROW_TILE

    xs = jnp.concatenate([ctx, x], axis=1)
    mods = _modulation(c, c_ctx, ada_w, ada_b)
    cos, sin = _rope_tables(seq, ctx_len)
    lb_sm = jax.nn.softmax(hg_lb_logits.astype(F32), axis=0)
    lower_bounds = jnp.clip(jnp.cumsum(lb_sm, axis=0) - lb_sm[0], 0.0, LB_CEIL)
    w_in_pad = jnp.pad(w_in, ((0, 0), (0, 0), (0, D_IN_PAD - D_IN))).astype(BF16)
    s5_ops = _s5_operators(s5_lam_re, s5_lam_im, s5_log_dt, s5_b_re, s5_b_im, s5_c_re, s5_c_im)
    zeros = jnp.zeros((depth, 3, MIX_W), F32)
    lb_rows = jnp.concatenate([jnp.maximum(lower_bounds, LB_FLOOR)[:, None], (1.0 - lower_bounds)[:, None],
                               zeros, zeros], axis=1)
    vecs = jnp.stack([ret_gn_w, hg_norm_w, gla_norm_w, s5_d, s5_glu_b], axis=1)
    vecs = jnp.concatenate([vecs, zeros], axis=1)
    wa_pad = jnp.stack([jnp.pad(gla_wa2[:, dr], ((0, 0), (GLA_RANK * dr, 128 - GLA_RANK * (dr + 1)), (0, 0)))
                        for dr in range(2)], axis=1)
    w_out16, glu16 = w_out.astype(BF16), s5_glu_w.astype(BF16)

    for l in range(depth):
        p_ret, p_s5, p_hg, p_gla = _project(xs, mods[l], norm1_w[l], w_in_pad[l], n_ctx_tiles)

        yrf = _ret_scan(p_ret, cos, sin, n_ctx_tiles, False)
        yrb = _ret_scan(p_ret, cos, sin, n_ctx_tiles, True)
        ys5 = _s5_mix(p_s5, *(o[l] for o in s5_ops), ctx_len)
        yhf = _hg_scan(p_hg, lb_rows[l], n_ctx_tiles, False)
        yhb = _hg_scan(p_hg, lb_rows[l], n_ctx_tiles, True)
        ygs = [_gla_scan(p_gla, wa_pad[l, dr], gla_ba[l, dr][None, :], n_ctx_tiles, bool(dr)) for dr in range(2)]

        j = l // 2
        router = (router_w[j], router_b[j]) if l % 2 == 1 else None
        outs = _mix_out(xs, mods[l], p_ret, yrf, yrb, p_s5, ys5, p_hg, yhf, yhb, p_gla, ygs[0], ygs[1],
                        vecs[l], glu16[l], w_out16[l], norm2_w[l], router, n_ctx_tiles)
        if l % 2 == 0:
            x_new, tok = outs
            xs = _dense_ffn(x_new, tok, mods[l], ffn_w1[j], ffn_w3[j], ffn_w2[j], n_ctx_tiles)
        else:
            x_new, tok, logits = outs
            n_tok = bsz * xs.shape[1]
            dest, gates, row_tok, block_e = _routing(logits.reshape(n_tok, 128))
            ybuf = _moe_experts(tok.reshape(n_tok, d), row_tok, block_e, moe_w1[j], moe_w3[j], moe_w2[j])
            xs = _moe_combine(x_new, mods[l], ybuf, dest, gates, n_ctx_tiles)

    return _final_norm(xs, final_norm_w, n_ctx_tiles)
```

```python
import functools
import math

import numpy as np
import jax
import jax.numpy as jnp
from jax import lax
from jax.experimental import pallas as pl
from jax.experimental.pallas import tpu as pltpu

F32 = jnp.float32
BF16 = jnp.bfloat16

D_MODEL = 1024
GRID_W = 64
MIX_W = 256
NORM_EPS = 1e-6
ROPE_BASE = 10000.0
LB_FLOOR = 1e-30
LB_CEIL = 1.0 - 1e-6
N_HEADS = 4
S5_GROUP = 16
S5_GROUPS = 16
S5_STATE = 64
S5_SUB = 16
GLA_RANK = 16
GLA_TAU = 16.0
N_EXPERTS = 8
TOP_K = 2
D_IN = 3360
D_IN_PAD = 3456

SCAN_CHUNK = 256
SCAN_SUB = 16
TAME_LIMIT = 40.0
NEG_BIG = -1e30
VMEM_LIMIT = 56 * 1024 * 1024


def _cparams(sem):
    return pltpu.CompilerParams(dimension_semantics=sem, vmem_limit_bytes=VMEM_LIMIT)


def _full(shape):
    n = len(shape)
    return pl.BlockSpec(shape, lambda *_: (0,) * n)


def _dot(a, b):
    return jnp.dot(a, b, preferred_element_type=F32)


def _dot_nt(a, b):
    return lax.dot_general(a, b, (((1,), (1,)), ((), ())), preferred_element_type=F32)


def _dot_hi(a, b):
    return jnp.dot(a, b, preferred_element_type=F32, precision=lax.Precision.HIGHEST)


def _sigmoid(z):
    return 0.5 * (jnp.tanh(0.5 * z) + 1.0)


def _silu(z):
    return z * _sigmoid(z)


def _log_sigmoid(z):
    return jnp.minimum(z, 0.0) - jnp.log(1.0 + jnp.exp(-jnp.abs(z)))


def _scan_constants(n_heads, wk, wv, reverse):
    c, s, hc = SCAN_CHUNK, SCAN_SUB, SCAN_CHUNK // 2
    i = np.arange(c)[:, None]
    r = np.arange(c)[None, :]
    cum = (r >= i) if reverse else (r <= i)
    ih = np.arange(hc)[:, None]
    jh = np.arange(hc)[None, :]
    causal = (jh >= ih) if reverse else (jh <= ih)
    masks = [((ih // s) == (jh // s)) & causal]
    m = s
    while m < hc:
        same = (ih // (2 * m)) == (jh // (2 * m))
        i_hi, j_hi = (ih % (2 * m)) >= m, (jh % (2 * m)) >= m
        masks.append(same & (~i_hi & j_hi if reverse else i_hi & ~j_hi))
        m *= 2
    dk, dv = wk // n_heads, wv // n_heads
    lk, lv = np.arange(wk), np.arange(wv)
    hmv = (lv[None, :] // dv) == np.arange(n_heads)[:, None]
    bd = (lv[:, None] // dv) == (lk[None, :] // dk)
    return dict(
        cm=jnp.asarray(np.concatenate([cum] * 2, axis=1), BF16),
        pm=jnp.asarray(np.stack([np.tile(x, (1, n_heads)) for x in masks]), F32),
        hmv=jnp.asarray(hmv[:, None, :], BF16),
        bd=jnp.asarray(bd, F32),
        gmat=jnp.asarray(bd.T, BF16),
    )


def _split3(g):
    hi = g.astype(BF16)
    r1 = g - hi.astype(F32)
    mid = r1.astype(BF16)
    lo = (r1 - mid.astype(F32)).astype(BF16)
    return hi, mid, lo


def _head_scores(qd, k16, n_heads):
    r, wk = qd.shape
    dk = wk // n_heads
    per_col = 128 // dk
    lane_head = lax.broadcasted_iota(jnp.int32, (1, 128), 1) // dk
    blocks = []
    for col in range(wk // 128):
        qc = qd[:, col * 128:(col + 1) * 128]
        lhs = jnp.concatenate([jnp.where(lane_head == h, qc, 0.0) for h in range(per_col)], axis=0)
        sc = _dot_nt(lhs.astype(BF16), k16[:, col * 128:(col + 1) * 128])
        blocks += [sc[h * r:(h + 1) * r, :] for h in range(per_col)]
    return jnp.concatenate(blocks, axis=1)


def _ref_rows(bb_ref, rows, span):
    wk = bb_ref.shape[1]
    pieces = [jnp.zeros((span, wk), F32) if r is None else jnp.broadcast_to(bb_ref[r:r + 1, :], (span, wk))
              for r in rows]
    return pieces[0] if len(pieces) == 1 else jnp.concatenate(pieces, axis=0)


def _scan_core(q, k, v, g, cm_ref, pm_ref, hmv_ref, bd_ref, st_ref, bb_ref, o_ref, *, reverse):
    c, s, hc = SCAN_CHUNK, SCAN_SUB, SCAN_CHUNK // 2
    wk, wv = q.shape[1], v.shape[1]
    n_heads = hmv_ref.shape[0]
    g_hi = g.astype(BF16)
    g_lo = (g - g_hi.astype(F32)).astype(BF16)
    b = _dot(cm_ref[...], jnp.concatenate([g_hi, g_lo], axis=0))
    bb_ref[...] = b
    end = 0 if reverse else c - 1
    b_end = bb_ref[end:end + 1, :]
    st = st_ref[...]
    o_inter = _dot_nt((q * jnp.exp(b)).astype(BF16), st.astype(BF16))
    upd = _dot(v.T.astype(BF16), (k * jnp.exp(b_end - b)).astype(BF16))
    st_ref[...] = st * jnp.exp(b_end) + upd * bd_ref[...]

    v16 = v.astype(BF16)
    vexp = [_expand_heads(v16[h0:h0 + hc], hmv_ref[...]) for h0 in (0, hc)]
    halves = (slice(0, hc), slice(hc, c))

    n_blk = c // s
    if reverse:
        rows = [(i + 1) * s for i in range(n_blk - 1)] + [None]
    else:
        rows = [None] + [i * s - 1 for i in range(1, n_blk)]
    dq = b - _ref_rows(bb_ref, rows, s)
    low = jnp.min(jnp.min(dq, axis=1, keepdims=True), axis=0, keepdims=True)
    qd = q * jnp.exp(dq)
    k16 = (k * jnp.exp(jnp.minimum(-dq, TAME_LIMIT))).astype(BF16)
    pm_diag = pm_ref[0] * (low >= -TAME_LIMIT).astype(F32)
    a = [_head_scores(qd[h], k16[h], n_heads) * pm_diag for h in halves]

    m, lev = s, 1
    while m < hc:
        n_pair = c // (2 * m)
        rows = [p * 2 * m + (m if reverse else m - 1) for p in range(n_pair)]
        ed = jnp.exp(-jnp.abs(b - _ref_rows(bb_ref, rows, 2 * m)))
        qd, k16 = q * ed, (k * ed).astype(BF16)
        a = [a[i] + _head_scores(qd[h], k16[h], n_heads) * pm_ref[lev] for i, h in enumerate(halves)]
        m, lev = 2 * m, lev + 1

    ed = jnp.exp(-jnp.abs(b - _ref_rows(bb_ref, [hc if reverse else hc - 1], c)))
    qd, k16 = q * ed, (k * ed).astype(BF16)
    late, early = (0, 1) if reverse else (1, 0)
    top = _head_scores(qd[halves[late]], k16[halves[early]], n_heads)

    o_ref[halves[early], :] = o_inter[halves[early]] + _dot(a[early].astype(BF16), vexp[early])
    a_late = jnp.concatenate([a[late], top], axis=1).astype(BF16)
    o_ref[halves[late], :] = o_inter[halves[late]] + _dot(a_late, jnp.concatenate([vexp[late], vexp[early]], axis=0))

    return low[0, 0] < -TAME_LIMIT


def _explicit_diagonal(q, k, v, gmat_ref, bb_ref, qb_ref, kb_ref, vb_ref, wj_ref, o_ref, reverse):
    c, s = SCAN_CHUNK, SCAN_SUB
    wv = v.shape[1]
    qb_ref[...] = q
    kb_ref[...] = k
    vb_ref[...] = v
    rows = lax.broadcasted_iota(jnp.int32, (s, 1), 0)

    def blk(ib, carry):
        base = pl.multiple_of(ib * s, s)
        bblk = bb_ref[pl.ds(base, s), :]
        qblk = qb_ref[pl.ds(base, s), :]
        for jo in range(s):
            bj = bb_ref[pl.ds(base + jo, 1), :]
            kj = kb_ref[pl.ds(base + jo, 1), :]
            valid = (rows <= jo) if reverse else (rows >= jo)
            w = qblk * kj * jnp.exp(jnp.where(valid, bblk - bj, NEG_BIG))
            wj_ref[pl.ds(jo * s, s), :] = w.astype(BF16)
        r = _dot(wj_ref[...], gmat_ref[...])
        acc = jnp.zeros((s, wv), F32)
        for jo in range(s):
            acc = acc + r[jo * s:(jo + 1) * s, :] * vb_ref[pl.ds(base + jo, 1), :]
        o_ref[pl.ds(base, s), :] += acc
        return carry

    lax.fori_loop(0, c // s, blk, 0)


def _rope(t, cos, sin_signed):
    outs = []
    for h in range(t.shape[1] // 128):
        th = t[:, h * 128:(h + 1) * 128]
        lane = lax.broadcasted_iota(jnp.int32, th.shape, 1)
        partner = jnp.where((lane % 32) < 16, pltpu.roll(th, 128 - 16, 1), pltpu.roll(th, 16, 1))
        outs.append(th * cos[:, h * 128:(h + 1) * 128] + partner * sin_signed[:, h * 128:(h + 1) * 128])
    return jnp.concatenate(outs, axis=1)


def _expand_heads(x16, hm):
    return (x16[None] * hm).reshape(hm.shape[0] * x16.shape[0], x16.shape[1])


def _ret_constants(reverse):
    c, hc = SCAN_CHUNK, SCAN_CHUNK // 2
    lg = jnp.log1p(-(2.0 ** (-5.0 - jnp.arange(N_HEADS, dtype=F32))))
    lg = lg[::-1] if reverse else lg
    lane = jnp.repeat(lg, MIX_W // N_HEADS)[None, :]
    i = jnp.arange(c, dtype=F32)[:, None]
    steps_in = (c - i) if reverse else (i + 1.0)
    steps_out = i if reverse else (c - 1.0 - i)
    dec = jnp.concatenate([jnp.exp(lane * steps_in), jnp.exp(lane * steps_out),
                           jnp.broadcast_to(jnp.exp(lane * c), (8, MIX_W))], axis=0)
    ih = jnp.arange(hc, dtype=F32)[:, None]
    jh = jnp.arange(hc, dtype=F32)[None, :]
    dist = (jh - ih) if reverse else (ih - jh)
    per_head = lambda d: jnp.concatenate(
        [jnp.where(d >= 0, jnp.exp(lg[h] * jnp.maximum(d, 0.0)), 0.0) for h in range(N_HEADS)], axis=1)
    dm = jnp.stack([per_head(dist), per_head(dist + hc)])
    hd = np.arange(MIX_W) // (MIX_W // N_HEADS)
    hm = jnp.asarray((hd[None, :] == np.arange(N_HEADS)[:, None])[:, None, :], BF16)
    bd = jnp.asarray(hd[:, None] == hd[None, :], F32)
    return dec, dm, hm, bd


SCAN_BATCH = 2


def _ret_kernel(p_ref, cos_ref, sin_ref, dec_ref, dm_ref, hm_ref, bd_ref, o_ref, st_ref, *, reverse):
    @pl.when(pl.program_id(1) == 0)
    def _():
        st_ref[...] = jnp.zeros_like(st_ref)

    c, hc = SCAN_CHUNK, SCAN_CHUNK // 2
    cos, sin = cos_ref[...], sin_ref[...]
    hm = hm_ref[...]
    halves = (slice(0, hc), slice(hc, c))
    late, early = (0, 1) if reverse else (1, 0)
    for bi in range(p_ref.shape[0]):
        q = _rope(p_ref[bi, :, 0:256], cos, sin)
        k = _rope(p_ref[bi, :, 256:512], cos, sin) * (64.0 ** -0.5)
        v = p_ref[bi, :, 512:768]
        st = st_ref[bi]
        o_inter = _dot_nt((q * dec_ref[0:c, :]).astype(BF16), st.astype(BF16))
        upd = _dot(v.T.astype(BF16), (k * dec_ref[c:2 * c, :]).astype(BF16))
        st_ref[bi] = st * dec_ref[2 * c:2 * c + 1, :] + upd * bd_ref[...]

        k16, v16 = k.astype(BF16), v.astype(BF16)
        vx = [_expand_heads(v16[h], hm) for h in halves]
        a_early = _head_scores(q[halves[early]], k16[halves[early]], N_HEADS) * dm_ref[0]
        a_late = _head_scores(q[halves[late]], k16[halves[late]], N_HEADS) * dm_ref[0]
        top = _head_scores(q[halves[late]], k16[halves[early]], N_HEADS) * dm_ref[1]
        o_ref[bi, halves[early], :] = o_inter[halves[early]] + _dot(a_early.astype(BF16), vx[early])
        a_late = jnp.concatenate([a_late, top], axis=1).astype(BF16)
        o_ref[bi, halves[late], :] = (o_inter[halves[late]]
                                      + _dot(a_late, jnp.concatenate([vx[late], vx[early]], axis=0)))


def _hg_inputs(p_ref, extra, reverse):
    (lb_ref,) = extra
    q = p_ref[:, 0:256]
    z = p_ref[:, 512:768] if reverse else p_ref[:, 256:512]
    v = p_ref[:, 768:1024]
    lb_floor, one_m_lb = lb_ref[0:1, :], lb_ref[1:2, :]
    sg = _sigmoid(z)
    return q, one_m_lb * (1.0 - sg), v, jnp.log(lb_floor + one_m_lb * sg)


def _gla_inputs(p_ref, extra, reverse):
    wa_ref, ba_ref = extra
    q = p_ref[:, 0:128]
    k = p_ref[:, 128:256] * (32.0 ** -0.5)
    v = p_ref[:, 256:512]
    g = _log_sigmoid(_dot_hi(p_ref[:, 768:896], wa_ref[...]) + ba_ref[...]) * (1.0 / GLA_TAU)
    return q, k, v, g


def _scan_kernel(p_ref, *rest, inputs_fn, n_extra, reverse):
    extra, rest = rest[:n_extra], rest[n_extra:]
    (cm_ref, pm_ref, hmv_ref, bd_ref, gmat_ref, o_ref, st_ref, bb_ref, *fallback_scratch) = rest

    @pl.when(pl.program_id(1) == 0)
    def _():
        st_ref[...] = jnp.zeros_like(st_ref)

    n_b = p_ref.shape[0]
    untame = []
    for bi in range(n_b):
        q, k, v, g = inputs_fn(p_ref.at[bi], extra, reverse)
        untame.append(_scan_core(q, k, v, g, cm_ref, pm_ref, hmv_ref, bd_ref,
                                 st_ref.at[bi], bb_ref.at[bi], o_ref.at[bi], reverse=reverse))
    for bi in range(n_b):
        @pl.when(untame[bi])
        def _(bi=bi):
            q, k, v, _ = inputs_fn(p_ref.at[bi], extra, reverse)
            _explicit_diagonal(q, k, v, gmat_ref, bb_ref.at[bi], *fallback_scratch, o_ref.at[bi], reverse)


def _scan_tile_map(n_tiles, n_ctx_tiles, reverse):
    if not reverse:
        return lambda step: step
    return lambda step: jnp.where(step < n_ctx_tiles, n_ctx_tiles - 1 - step, n_tiles - 1 - (step - n_ctx_tiles))


def _scan_call(inputs_fn, p, extra, wk, wv, n_ctx_tiles, reverse):
    bsz, t_all, wp = p.shape
    c, s, nb = SCAN_CHUNK, SCAN_SUB, math.gcd(SCAN_BATCH, bsz)
    n_tiles = t_all // c
    cst = _scan_constants(N_HEADS, wk, wv, reverse)
    tile = _scan_tile_map(n_tiles, n_ctx_tiles, reverse)
    consts = [cst[n] for n in ('cm', 'pm', 'hmv', 'bd', 'gmat')]
    in_specs = [pl.BlockSpec((nb, c, wp), lambda b, t: (b, tile(t), 0))]
    in_specs += [_full(x.shape) for x in extra] + [_full(x.shape) for x in consts]
    return pl.pallas_call(
        functools.partial(_scan_kernel, inputs_fn=inputs_fn, n_extra=len(extra), reverse=reverse),
        grid=(bsz // nb, n_tiles),
        in_specs=in_specs,
        out_specs=pl.BlockSpec((nb, c, wv), lambda b, t: (b, tile(t), 0)),
        out_shape=jax.ShapeDtypeStruct((bsz, t_all, wv), F32),
        scratch_shapes=[
            pltpu.VMEM((nb, wv, wk), F32), pltpu.VMEM((nb, c, wk), F32),
            pltpu.VMEM((c, wk), F32), pltpu.VMEM((c, wk), F32), pltpu.VMEM((c, wv), F32),
            pltpu.VMEM((s * s, wk), BF16),
        ],
        compiler_params=_cparams(("arbitrary", "arbitrary")),
    )(p, *extra, *consts)


def _ret_scan(p_ret, cos, sin, n_ctx_tiles, reverse):
    bsz, t_all, wp = p_ret.shape
    c, nb = SCAN_CHUNK, math.gcd(SCAN_BATCH, bsz)
    tile = _scan_tile_map(t_all // c, n_ctx_tiles, reverse)
    consts = _ret_constants(reverse)
    rows = lambda w: pl.BlockSpec((c, w), lambda b, t: (tile(t), 0))
    return pl.pallas_call(
        functools.partial(_ret_kernel, reverse=reverse),
        grid=(bsz // nb, t_all // c),
        in_specs=[pl.BlockSpec((nb, c, wp), lambda b, t: (b, tile(t), 0)), rows(MIX_W), rows(MIX_W)]
        + [_full(x.shape) for x in consts],
        out_specs=pl.BlockSpec((nb, c, MIX_W), lambda b, t: (b, tile(t), 0)),
        out_shape=jax.ShapeDtypeStruct((bsz, t_all, MIX_W), F32),
        scratch_shapes=[pltpu.VMEM((nb, MIX_W, MIX_W), F32)],
        compiler_params=_cparams(("arbitrary", "arbitrary")),
    )(p_ret, cos, sin, *consts)


def _hg_scan(p_hg, lb_rows, n_ctx_tiles, reverse):
    return _scan_call(_hg_inputs, p_hg, [lb_rows], 256, 256, n_ctx_tiles, reverse)


def _gla_scan(p_gla, wa, ba, n_ctx_tiles, reverse):
    return _scan_call(_gla_inputs, p_gla, [wa, ba], 128, 256, n_ctx_tiles, reverse)


def _rope_tables(seq, ctx_len):
    t_idx = jnp.arange(seq, dtype=jnp.int32)
    row = (t_idx // GRID_W).astype(F32)
    col = (t_idx % GRID_W).astype(F32)
    freqs = ROPE_BASE ** (-jnp.arange(16, dtype=F32) / 16)
    ang_r = row[:, None] * freqs[None, :]
    ang_c = col[:, None] * freqs[None, :]
    cos = jnp.concatenate([jnp.cos(ang_r)] * 2 + [jnp.cos(ang_c)] * 2, axis=1)
    sin = jnp.concatenate([-jnp.sin(ang_r), jnp.sin(ang_r), -jnp.sin(ang_c), jnp.sin(ang_c)], axis=1)
    cos = jnp.concatenate([jnp.ones((ctx_len, 64), F32), cos], axis=0)
    sin = jnp.concatenate([jnp.zeros((ctx_len, 64), F32), sin], axis=0)
    return jnp.tile(cos, (1, N_HEADS)), jnp.tile(sin, (1, N_HEADS))


def _s5_response_kernel(k_ref, rep_ref, o_ref):
    n = S5_SUB
    size = n * S5_GROUP
    diff = (lax.broadcasted_iota(jnp.int32, (size, size), 1) // S5_GROUP
            - lax.broadcasted_iota(jnp.int32, (size, size), 0) // S5_GROUP)

    def tiled(block):
        return jnp.concatenate([_dot_hi(block, rep_ref[...])] * n, axis=0)

    m = jnp.where(diff == 0, tiled(k_ref[0, 0] + k_ref[1, 0]), 0.0)
    for tau in range(1, n):
        m = jnp.where(diff == tau, tiled(k_ref[0, tau]), m)
        m = jnp.where(diff == -tau, tiled(k_ref[1, tau]), m)
    o_ref[...] = m.astype(o_ref.dtype)


def _s5_response(kern):
    depth, _, g, n, cg, _ = kern.shape
    rep = jnp.asarray(np.tile(np.eye(cg), (1, n)), F32)
    return pl.pallas_call(
        _s5_response_kernel, grid=(depth, g),
        in_specs=[pl.BlockSpec((None, 2, None, n, cg, cg), lambda l, gi: (l, 0, gi, 0, 0, 0)), _full(rep.shape)],
        out_specs=pl.BlockSpec((None, None, n * cg, n * cg), lambda l, gi: (l, gi, 0, 0)),
        out_shape=jax.ShapeDtypeStruct((depth, g, n * cg, n * cg), BF16),
        compiler_params=_cparams(("arbitrary", "arbitrary")),
    )(kern, rep)


def _s5_operators(lam_re, lam_im, log_dt, b_re, b_im, c_re, c_im):
    n = S5_SUB
    lam = lax.complex(lam_re, lam_im)
    log_a = lam * jnp.exp(log_dt)[..., None]
    a_bar = jnp.exp(log_a)
    b_bar = ((a_bar - 1) / lam)[..., None] * lax.complex(b_re, b_im)
    c_mat = lax.complex(c_re, c_im)
    step = np.arange(n)
    pw = jnp.exp(log_a[..., None, :] * jnp.arange(n + 1, dtype=F32)[:, None])
    kern = jnp.einsum('...cp,...tp,...pd->...tdc', c_mat, pw[..., :n, :], b_bar).real
    m = _s5_response(kern)
    qc =pw[..., n - 1 - step, :][..., None] * b_bar[..., None, :, :]
    qc = jnp.swapaxes(qc, -1, -2)
    z = c_mat[..., None, :, :] * pw[..., step + 1, :][..., None, :]
    z = jnp.moveaxis(z, -1, -3)
    qc = jnp.stack([qc[:, 0], jnp.flip(qc[:, 1], axis=-3)], axis=1)
    z = jnp.stack([z[:, 0], jnp.flip(z[:, 1], axis=-2)], axis=1)
    lead = qc.shape[:3]
    qc =qc.reshape(*lead, n * S5_GROUP, S5_STATE)
    z = z.reshape(*lead, S5_STATE, n * S5_GROUP)
    q = jnp.concatenate([qc.real, qc.imag, qc.imag, qc.real], axis=-1)
    pm = jnp.concatenate([z.real, -z.imag, jnp.zeros_like(z.real), jnp.zeros_like(z.real)], axis=-2)
    a_row = pw[..., n, :]
    a1 = jnp.concatenate([a_row.real] * 4, axis=-1)
    a2 = jnp.concatenate([-a_row.imag, a_row.imag, a_row.imag, -a_row.imag], axis=-1)
    a = jnp.stack([a1[:, 0], a2[:, 0], a1[:, 1], a2[:, 1]], axis=1)
    return (m, jnp.concatenate([q[:, 0], q[:, 1]], axis=-1).astype(BF16),
            pm[:, 0].astype(BF16), pm[:, 1].astype(BF16), a)


S5_GROUP_BLOCK = 8


S5_TILE = S5_SUB * S5_SUB


def _s5_in_kernel(u_ref, q_ref, wf_ref, wb_ref, uf_ref):
    gl = S5_GROUP_BLOCK * S5_GROUP

    def regroup(t, carry):
        x3 = u_ref[pl.ds(pl.multiple_of(t * S5_TILE, S5_TILE), S5_TILE), :].astype(BF16).reshape(S5_SUB, S5_SUB, gl)
        for k in range(S5_GROUP_BLOCK):
            piece = x3[:, :, k * S5_GROUP:(k + 1) * S5_GROUP].reshape(S5_SUB, S5_SUB * S5_GROUP)
            uf_ref[k, pl.ds(pl.multiple_of(t * S5_SUB, S5_SUB), S5_SUB), :] = piece
        return carry

    lax.fori_loop(0, u_ref.shape[0] // S5_TILE, regroup, 0)
    for k in range(S5_GROUP_BLOCK):
        w = _dot(uf_ref[k], q_ref[k])
        wf_ref[:, k, :] = w[:, :256]
        wb_ref[:, k, :] = w[:, 256:]


def _s5_state_kernel(wf_ref, wb_ref, a_ref, hf_ref, hb_ref, sf_ref, sb_ref):
    @pl.when(pl.program_id(0) == 0)
    def _():
        sf_ref[...] = jnp.zeros_like(sf_ref)
        sb_ref[...] = jnp.zeros_like(sb_ref)

    def swap(h):
        return jnp.concatenate([h[:, 128:], h[:, :128]], axis=1)

    n = wf_ref.shape[0]
    hf, hb = sf_ref[...], sb_ref[...]
    a1f, a2f, a1b, a2b = a_ref[0], a_ref[1], a_ref[2], a_ref[3]
    for i in range(n):
        hf_ref[i] = hf
        hf = a1f * hf + a2f * swap(hf) + wf_ref[i]
        j = n - 1 - i
        hb_ref[j] = hb
        hb = a1b * hb + a2b * swap(hb) + wb_ref[j]
    sf_ref[...] = hf
    sb_ref[...] = hb


def _s5_out_kernel(u_ref, m_ref, hf_ref, hb_ref, pf_ref, pb_ref, y_ref, ys_ref, hs_ref):
    for k in range(S5_GROUP_BLOCK):
        hs_ref[k % 2, 0] = hf_ref[:, k, :]
        hs_ref[k % 2, 1] = hb_ref[:, k, :]
        y = _dot(u_ref[k], m_ref[k])
        y = y + _dot(hs_ref[k % 2, 0].astype(BF16), pf_ref[k])
        y = y + _dot(hs_ref[k % 2, 1].astype(BF16), pb_ref[k])
        ys_ref[k] = y

    def regroup(t, carry):
        rows = pl.ds(pl.multiple_of(t * S5_SUB, S5_SUB), S5_SUB)
        pieces = [ys_ref[k, rows, :].reshape(S5_SUB, S5_SUB, S5_GROUP) for k in range(S5_GROUP_BLOCK)]
        y_ref[pl.ds(pl.multiple_of(t * S5_TILE, S5_TILE), S5_TILE), :] = (
            jnp.concatenate(pieces, axis=-1).reshape(S5_TILE, S5_GROUP_BLOCK * S5_GROUP))
        return carry

    lax.fori_loop(0, y_ref.shape[0] // S5_TILE, regroup, 0)


def _s5_mix(u, m, q, pf, pb, a_rows, ctx_len):
    bsz, t_all, _ = u.shape
    n, g = S5_SUB, S5_GROUPS
    n_rows = t_all // n
    blk = 16
    n_steps = n_rows // blk
    n_ctx = (ctx_len // n) // blk
    gb = S5_GROUP_BLOCK
    n_gb = g // gb
    nat_spec = pl.BlockSpec((None, t_all, gb * S5_GROUP), lambda b, gi: (b, 0, gi))
    u_spec = pl.BlockSpec((None, gb, n_rows, 256), lambda b, gi: (b, gi, 0, 0))
    st_spec = pl.BlockSpec((n_rows, gb, 256), lambda b, gi: (0, b * n_gb + gi, 0))
    hshape = jax.ShapeDtypeStruct((n_rows, bsz * g, 256), F32)
    wf, wb, uf = pl.pallas_call(
        _s5_in_kernel, grid=(bsz, n_gb),
        in_specs=[nat_spec, pl.BlockSpec((gb, 256, 512), lambda b, gi: (gi, 0, 0))],
        out_specs=[st_spec, st_spec, u_spec],
        out_shape=[hshape, hshape, jax.ShapeDtypeStruct((bsz, g, n_rows, 256), BF16)],
        compiler_params=_cparams(("arbitrary", "arbitrary")),
    )(u, q)

    a = jnp.tile(a_rows, (1, bsz, 1))
    fwd = lambda s: (s, 0, 0)
    bwd = lambda s: (jnp.where(s < n_ctx, n_ctx - 1 - s, n_steps - 1 - (s - n_ctx)), 0, 0)
    st_block = (blk, bsz * g, 256)
    hf, hb = pl.pallas_call(
        _s5_state_kernel, grid=(n_steps,),
        in_specs=[pl.BlockSpec(st_block, fwd), pl.BlockSpec(st_block, bwd), _full(a.shape)],
        out_specs=[pl.BlockSpec(st_block, fwd), pl.BlockSpec(st_block, bwd)],
        out_shape=[hshape, hshape],
        scratch_shapes=[pltpu.VMEM((bsz * g, 256), F32), pltpu.VMEM((bsz * g, 256), F32)],
        compiler_params=_cparams(("arbitrary",)),
    )(wf, wb, a)

    mat_spec = pl.BlockSpec((gb, 256, 256), lambda b, gi: (gi, 0, 0))
    return pl.pallas_call(
        _s5_out_kernel, grid=(bsz, n_gb),
        in_specs=[u_spec, mat_spec, st_spec, st_spec, mat_spec, mat_spec],
        out_specs=nat_spec,
        out_shape=jax.ShapeDtypeStruct((bsz, t_all, g * S5_GROUP), F32),
        scratch_shapes=[pltpu.VMEM((gb, n_rows, 256), F32), pltpu.VMEM((2, 2, n_rows, 256), F32)],
        compiler_params=_cparams(("arbitrary", "arbitrary")),
    )(uf, m, hf, hb, pf, pb)


ROW_TILE = 256
M_SHIFT1, M_SCALE1, M_GATE1, M_SHIFT2, M_SCALE2, M_GATE2 = range(6)


def _ada_kernel(cond_ref, w_ref, b_ref, o_ref):
    o_ref[...] = _dot_hi(_silu(cond_ref[...]), w_ref[...]) + b_ref[...]


def _modulation(c, c_ctx, ada_w, ada_b):
    depth, d, d6 = ada_w.shape
    bsz = c.shape[0]
    cond = jnp.zeros((8, d), F32).at[:bsz].set(c).at[bsz].set(c_ctx)
    out = pl.pallas_call(
        _ada_kernel, grid=(depth, d6 // d),
        in_specs=[_full((8, d)),
                  pl.BlockSpec((None, d, d), lambda l, j: (l, 0, j)),
                  pl.BlockSpec((None, 1, d), lambda l, j: (l, 0, j))],
        out_specs=pl.BlockSpec((None, 8, d), lambda l, j: (l, 0, j)),
        out_shape=jax.ShapeDtypeStruct((depth, 8, d6), F32),
        compiler_params=_cparams(("arbitrary", "arbitrary")),
    )(cond, ada_w, ada_b[:, None, :])
    out = out.reshape(depth, 8, 6, d)
    lat = out[:, :bsz]
    ctx = jnp.broadcast_to(out[:, bsz][:, None], lat.shape)
    mods = jnp.stack([ctx, lat], axis=2)
    return jnp.pad(mods, ((0, 0), (0, 0), (0, 0), (0, 2), (0, 0)))


def _rms(x, w):
    return x * lax.rsqrt(jnp.mean(x * x, axis=-1, keepdims=True) + NORM_EPS) * w


def _proj_kernel(x_ref, mod_ref, nw_ref, w_ref, ret_ref, s5_ref, hg_ref, gla_ref):
    h = _rms(x_ref[...], nw_ref[...]) * (1.0 + mod_ref[M_SCALE1:M_SCALE1 + 1, :]) + mod_ref[M_SHIFT1:M_SHIFT1 + 1, :]
    p = _dot(h.astype(BF16), w_ref[...])
    ret_ref[...] = p[:, 0:1024]
    s5_ref[...] = p[:, 1024:1280]
    hg_ref[...] = p[:, 1280:2560]
    gla_ref[...] = p[:, 2560:3456]


def _tile_specs(n_ctx_tiles):
    tm = ROW_TILE
    row = lambda w, j=0: pl.BlockSpec((None, tm, w), lambda b, t: (b, t, j))
    mod = pl.BlockSpec((None, None, 8, D_MODEL), lambda b, t: (b, jnp.where(t >= n_ctx_tiles, 1, 0), 0, 0))
    return row, mod


def _project(xs, mods, norm_w, w_pad, n_ctx_tiles):
    bsz, t_all, d = xs.shape
    row, mod = _tile_specs(n_ctx_tiles)
    widths = (1024, 256, 1280, 896)
    return pl.pallas_call(
        _proj_kernel, grid=(bsz, t_all // ROW_TILE),
        in_specs=[row(d), mod, _full((1, d)), _full(w_pad.shape)],
        out_specs=[row(w) for w in widths],
        out_shape=[jax.ShapeDtypeStruct((bsz, t_all, w), F32) for w in widths],
        compiler_params=_cparams(("arbitrary", "arbitrary")),
    )(xs, mods, norm_w[None, :], w_pad)


def _dot3(x, mat):
    hi, mid, lo = _split3(x)
    return _dot(hi, mat) + _dot(mid, mat) + _dot(lo, mat)


def _head_norm(y, gavg, w, center):
    if center:
        y = y - _dot3(y, gavg)
    return y * lax.rsqrt(_dot3(y * y, gavg) + NORM_EPS) * w


def _gelu_tanh(x):
    return 0.5 * x * (1.0 + jnp.tanh(math.sqrt(2.0 / math.pi) * (x + 0.044715 * (x * x * x))))


def _out_kernel(x_ref, mod_ref, gret_ref, yrf_ref, yrb_ref, u_ref, ys5_ref, ghg_ref, yhf_ref, yhb_ref,
                ggl_ref, ygf_ref, ygb_ref, vec_ref, glu_ref, gavg_ref, wo_ref, nw_ref, *rest, with_router):
    if with_router:
        rw_ref, rb_ref, xo_ref, tok_ref, lg_ref = rest
    else:
        xo_ref, tok_ref = rest
    gavg = gavg_ref[...]
    y_ret = _head_norm(yrf_ref[...] + yrb_ref[...], gavg, vec_ref[0:1, :], True) * _silu(gret_ref[...])
    u = u_ref[...]
    z = _gelu_tanh(ys5_ref[...] + vec_ref[3:4, :] * u)
    y_s5 = z * _sigmoid(_dot(z.astype(BF16), glu_ref[...]) + vec_ref[4:5, :])
    y_hg = _head_norm(yhf_ref[...] + yhb_ref[...], gavg, vec_ref[1:2, :], False) * _silu(ghg_ref[...])
    y_gla = _head_norm(ygf_ref[...] + ygb_ref[...], gavg, vec_ref[2:3, :], False) * _silu(ggl_ref[...])
    acc = _dot(y_ret.astype(BF16), wo_ref[0:256, :])
    acc = acc + _dot(y_s5.astype(BF16), wo_ref[256:512, :])
    acc = acc + _dot(y_hg.astype(BF16), wo_ref[512:768, :])
    acc = acc + _dot(y_gla.astype(BF16), wo_ref[768:1024, :])
    x_new = x_ref[...] + mod_ref[M_GATE1:M_GATE1 + 1, :] * acc
    xo_ref[...] = x_new
    tok = _rms(x_new, nw_ref[...]) * (1.0 + mod_ref[M_SCALE2:M_SCALE2 + 1, :]) + mod_ref[M_SHIFT2:M_SHIFT2 + 1, :]
    tok_ref[...] = tok
    if with_router:
        t_hi = tok.astype(BF16)
        t_lo = (tok - t_hi.astype(F32)).astype(BF16)
        lg_ref[...] = (_dot(t_hi, rw_ref[0]) + _dot(t_lo, rw_ref[0]) + _dot(t_hi, rw_ref[1])) + rb_ref[...]


def _mix_out(xs, mods, p_ret, yrf, yrb, p_s5, ys5, p_hg, yhf, yhb, p_gla, ygf, ygb,
             vec, glu_w, w_out, norm2_w, router, n_ctx_tiles):
    bsz, t_all, d = xs.shape
    row, mod = _tile_specs(n_ctx_tiles)
    head = np.arange(MIX_W) // (MIX_W // N_HEADS)
    gavg = jnp.asarray((head[:, None] == head[None, :]) / (MIX_W // N_HEADS), BF16)
    ins = [xs, mods, p_ret, yrf, yrb, p_s5, ys5, p_hg, yhf, yhb, p_gla, ygf, ygb,
           vec, glu_w.astype(BF16), gavg, w_out.astype(BF16), norm2_w[None, :]]
    specs = [row(d), mod, row(256, 3), row(256), row(256), row(256), row(256), row(256, 4), row(256), row(256),
             row(256, 2), row(256), row(256), _full(vec.shape), _full(glu_w.shape), _full(gavg.shape),
             _full(w_out.shape), _full((1, d))]
    out_specs = [row(d), row(d)]
    out_shape = [jax.ShapeDtypeStruct((bsz, t_all, d), F32), jax.ShapeDtypeStruct((bsz, t_all, d), F32)]
    if router is not None:
        rw = jnp.pad(router[0], ((0, 0), (0, 128 - N_EXPERTS)))
        rw_hi = rw.astype(BF16)
        rw = jnp.stack([rw_hi, (rw - rw_hi.astype(F32)).astype(BF16)])
        rb = jnp.pad(router[1], (0, 128 - N_EXPERTS))[None, :]
        ins += [rw, rb]
        specs += [_full(rw.shape), _full(rb.shape)]
        out_specs.append(row(128))
        out_shape.append(jax.ShapeDtypeStruct((bsz, t_all, 128), F32))
    return pl.pallas_call(
        functools.partial(_out_kernel, with_router=router is not None), grid=(bsz, t_all // ROW_TILE),
        in_specs=specs, out_specs=out_specs, out_shape=out_shape,
        compiler_params=_cparams(("arbitrary", "arbitrary")),
    )(*ins)


FF_SPLIT = 2


def _swiglu(tok16, w1_ref, w3_ref, w2_ref):
    d_ff = w1_ref.shape[-1]
    step = d_ff // FF_SPLIT
    acc = None
    for j in range(FF_SPLIT):
        sl = slice(j * step, (j + 1) * step)
        h = _silu(_dot(tok16, w1_ref[:, sl])) * _dot(tok16, w3_ref[:, sl])
        part = _dot(h.astype(BF16), w2_ref[sl, :])
        acc = part if acc is None else acc + part
    return acc


def _ffn_kernel(x_ref, tok_ref, mod_ref, w1_ref, w3_ref, w2_ref, o_ref):
    f = _swiglu(tok_ref[...].astype(BF16), w1_ref, w3_ref, w2_ref)
    o_ref[...] = x_ref[...] + mod_ref[M_GATE2:M_GATE2 + 1, :] * f


def _dense_ffn(xs, tok, mods, w1, w3, w2, n_ctx_tiles):
    bsz, t_all, d = xs.shape
    row, mod = _tile_specs(n_ctx_tiles)
    once = lambda a: pl.BlockSpec(a.shape, lambda b, t: (0, 0), pipeline_mode=pl.Buffered(1))
    w1, w3, w2 = w1.astype(BF16), w3.astype(BF16), w2.astype(BF16)
    return pl.pallas_call(
        _ffn_kernel, grid=(bsz, t_all // ROW_TILE),
        in_specs=[row(d), row(d), mod, once(w1), once(w3), once(w2)],
        out_specs=row(d), out_shape=jax.ShapeDtypeStruct(xs.shape, F32),
        compiler_params=_cparams(("arbitrary", "arbitrary")),
    )(xs, tok, mods, w1, w3, w2)


MOE_BLOCK = 512


def _routing(logits):
    n = logits.shape[0]
    top_v, top_i = lax.top_k(logits[:, :N_EXPERTS], TOP_K)
    gates = jax.nn.softmax(top_v, axis=-1)
    flat_e = top_i.reshape(-1).astype(jnp.int32)
    onehot = (flat_e[:, None] == jnp.arange(N_EXPERTS, dtype=jnp.int32)[None, :]).astype(jnp.int32)
    csum = jnp.cumsum(onehot, axis=0)
    counts = csum[-1]
    rank = jnp.take_along_axis(csum, flat_e[:, None], axis=1)[:, 0] - 1
    padded = (counts + MOE_BLOCK - 1) // MOE_BLOCK * MOE_BLOCK
    pends = jnp.cumsum(padded)
    dest = (pends - padded)[flat_e] + rank
    n_blocks = -(-(n * TOP_K) // MOE_BLOCK) + N_EXPERTS
    cap = n_blocks * MOE_BLOCK
    flat_tok = jnp.arange(n * TOP_K, dtype=jnp.int32) // TOP_K
    row_tok = jnp.zeros((cap,), jnp.int32).at[dest].set(flat_tok)
    block_start = jnp.arange(n_blocks, dtype=jnp.int32) * MOE_BLOCK
    block_e = jnp.minimum(jnp.searchsorted(pends, block_start, side='right'), N_EXPERTS - 1).astype(jnp.int32)
    return dest.reshape(n, TOP_K), gates, row_tok, block_e


def _row_copy(src_hbm, row, dst_ref, sem):
    return pltpu.make_async_copy(src_hbm.at[pl.ds(row, 1), :], dst_ref, sem)


def _rows_wait(src_hbm, dst_ref, sem):
    pltpu.make_async_copy(src_hbm.at[pl.ds(0, dst_ref.shape[0]), :], dst_ref, sem).wait()


def _moe_kernel(be_ref, idx_ref, idx_next_ref, tok_hbm, w1_ref, w3_ref, w2_ref, o_ref, xg0_ref, xg1_ref, sem):
    del be_ref
    i, j = pl.program_id(0), pl.program_id(1)
    bm = xg0_ref.shape[0]
    part_rows = bm // FF_SPLIT

    @pl.when((i == 0) & (j == 0))
    def _():
        def body(r, carry):
            _row_copy(tok_hbm, idx_ref[0, r], xg0_ref.at[pl.ds(r, 1), :], sem.at[0]).start()
            return carry
        lax.fori_loop(0, bm, body, 0)

    def step(cur_ref, nxt_ref, s_cur, s_nxt):
        @pl.when(j == 0)
        def _():
            _rows_wait(tok_hbm, cur_ref, sem.at[s_cur])

        base = j * part_rows
        for r in range(part_rows):
            _row_copy(tok_hbm, idx_next_ref[0, base + r], nxt_ref.at[pl.ds(base + r, 1), :], sem.at[s_nxt]).start()

        x16 = cur_ref[...].astype(BF16)
        h = _silu(_dot(x16, w1_ref[...])) * _dot(x16, w3_ref[...])
        part = _dot(h.astype(BF16), w2_ref[...])

        @pl.when(j == 0)
        def _():
            o_ref[...] = part

        @pl.when(j > 0)
        def _():
            o_ref[...] += part

        @pl.when((i == pl.num_programs(0) - 1) & (j == FF_SPLIT - 1))
        def _():
            _rows_wait(tok_hbm, nxt_ref, sem.at[s_nxt])

    @pl.when(i % 2 == 0)
    def _():
        step(xg0_ref, xg1_ref, 0, 1)

    @pl.when(i % 2 == 1)
    def _():
        step(xg1_ref, xg0_ref, 1, 0)


def _moe_experts(tok, row_tok, block_e, w1, w3, w2):
    n, d = tok.shape
    bm = MOE_BLOCK
    n_blocks = block_e.shape[0]
    d_ff = w1.shape[-1]
    step = d_ff // FF_SPLIT
    idx = row_tok.reshape(n_blocks, 1, bm)
    smem_blk = lambda f: pl.BlockSpec((None, 1, bm), f, memory_space=pltpu.SMEM)
    grid_spec = pltpu.PrefetchScalarGridSpec(
        num_scalar_prefetch=1, grid=(n_blocks, FF_SPLIT),
        in_specs=[
            smem_blk(lambda i, j, be: (i, 0, 0)),
            smem_blk(lambda i, j, be: (jnp.minimum(i + 1, n_blocks - 1), 0, 0)),
            pl.BlockSpec(memory_space=pl.ANY),
            pl.BlockSpec((None, d, step), lambda i, j, be: (be[i], 0, j)),
            pl.BlockSpec((None, d, step), lambda i, j, be: (be[i], 0, j)),
            pl.BlockSpec((None, step, d), lambda i, j, be: (be[i], j, 0)),
        ],
        out_specs=pl.BlockSpec((bm, d), lambda i, j, be: (i, 0)),
        scratch_shapes=[pltpu.VMEM((bm, d), F32), pltpu.VMEM((bm, d), F32), pltpu.SemaphoreType.DMA((2,))],
    )
    return pl.pallas_call(
        _moe_kernel, grid_spec=grid_spec,
        out_shape=jax.ShapeDtypeStruct((n_blocks * bm, d), F32),
        compiler_params=_cparams(("arbitrary", "arbitrary")),
    )(block_e, idx, idx, tok, w1.astype(BF16), w3.astype(BF16), w2.astype(BF16))


def _combine_kernel(d_ref, dn_ref, x_ref, g_ref, mod_ref, y_hbm, o_ref, buf_ref, sem):
    i = pl.program_id(0)
    slot = i % 2
    tm = x_ref.shape[0]

    def gather(idx, to_slot):
        def body(r, carry):
            for k in range(TOP_K):
                _row_copy(y_hbm, idx[k, r], buf_ref.at[to_slot, pl.ds(k * tm + r, 1), :], sem.at[to_slot]).start()
            return carry
        lax.fori_loop(0, tm, body, 0, unroll=8)

    @pl.when(i == 0)
    def _():
        gather(d_ref, 0)

    @pl.when(i + 1 < pl.num_programs(0))
    def _():
        gather(dn_ref, 1 - slot)

    _rows_wait(y_hbm, buf_ref.at[slot], sem.at[slot])
    f = buf_ref[slot, 0:tm, :] * g_ref[:, 0:1] + buf_ref[slot, tm:2 * tm, :] * g_ref[:, 1:2]
    o_ref[...] = x_ref[...] + mod_ref[M_GATE2:M_GATE2 + 1, :] * f


def _moe_combine(xs, mods, ybuf, dest, gates, n_ctx_tiles):
    bsz, t_all, d = xs.shape
    tm = ROW_TILE
    tiles_b = t_all // tm
    n_tiles = bsz * tiles_b
    dt = dest.reshape(n_tiles, tm, TOP_K).transpose(0, 2, 1)
    smem_blk = lambda f: pl.BlockSpec((None, TOP_K, tm), f, memory_space=pltpu.SMEM)
    out = pl.pallas_call(
        _combine_kernel, grid=(n_tiles,),
        in_specs=[
            smem_blk(lambda i: (i, 0, 0)),
            smem_blk(lambda i: (jnp.minimum(i + 1, n_tiles - 1), 0, 0)),
            pl.BlockSpec((tm, d), lambda i: (i, 0)),
            pl.BlockSpec((tm, TOP_K), lambda i: (i, 0)),
            pl.BlockSpec((None, None, 8, d),
                         lambda i: (i // tiles_b, jnp.where(i % tiles_b >= n_ctx_tiles, 1, 0), 0, 0)),
            pl.BlockSpec(memory_space=pl.ANY),
        ],
        out_specs=pl.BlockSpec((tm, d), lambda i: (i, 0)),
        out_shape=jax.ShapeDtypeStruct((bsz * t_all, d), F32),
        scratch_shapes=[pltpu.VMEM((2, TOP_K * tm, d), F32), pltpu.SemaphoreType.DMA((2,))],
        compiler_params=_cparams(("arbitrary",)),
    )(dt, dt, xs.reshape(bsz * t_all, d), gates, mods, ybuf)
    return out.reshape(bsz, t_all, d)


def _final_kernel(x_ref, w_ref, o_ref):
    o_ref[...] = _rms(x_ref[...], w_ref[...])


def _final_norm(xs, w, n_ctx_tiles):
    bsz, t_all, d = xs.shape
    tm = ROW_TILE
    n_lat = t_all // tm - n_ctx_tiles
    return pl.pallas_call(
        _final_kernel, grid=(bsz, n_lat),
        in_specs=[pl.BlockSpec((None, tm, d), lambda b, t: (b, t + n_ctx_tiles, 0)), _full((1, d))],
        out_specs=pl.BlockSpec((None, tm, d), lambda b, t: (b, t, 0)),
        out_shape=jax.ShapeDtypeStruct((bsz, n_lat * tm, d), F32),
        compiler_params=_cparams(("arbitrary", "arbitrary")),
    )(xs, w[None, :])


def kernel(x, c, ctx, c_ctx, ada_w, ada_b, norm1_w, norm2_w, w_in, w_out, ret_gn_w, s5_lam_re, s5_lam_im, s5_log_dt, s5_b_re, s5_b_im, s5_c_re, s5_c_im, s5_d, s5_glu_w, s5_glu_b, hg_lb_logits, hg_norm_w, gla_wa2, gla_ba, gla_norm_w, ffn_w1, ffn_w3, ffn_w2, router_w, router_b, moe_w1, moe_w3, moe_w2, final_norm_w):
    bsz, seq, d = x.shape
    ctx_len = ctx.shape[1]
    depth = ada_w.shape[0]
    assert d == D_MODEL and ctx_len % ROW_TILE == 0 and seq % ROW_TILE == 0
    assert ROW_TILE == SCAN_CHUNK and seq % GRID_W == 0
    n_ctx_tiles = ctx_len // ROW_TILE

    xs = jnp.concatenate([ctx, x], axis=1)
    mods = _modulation(c, c_ctx, ada_w, ada_b)
    cos, sin = _rope_tables(seq, ctx_len)
    lb_sm = jax.nn.softmax(hg_lb_logits.astype(F32), axis=0)
    lower_bounds = jnp.clip(jnp.cumsum(lb_sm, axis=0) - lb_sm[0], 0.0, LB_CEIL)
    w_in_pad = jnp.pad(w_in, ((0, 0), (0, 0), (0, D_IN_PAD - D_IN))).astype(BF16)
    s5_ops = _s5_operators(s5_lam_re, s5_lam_im, s5_log_dt, s5_b_re, s5_b_im, s5_c_re, s5_c_im)
    zeros = jnp.zeros((depth, 3, MIX_W), F32)
    lb_rows = jnp.concatenate([jnp.maximum(lower_bounds, LB_FLOOR)[:, None], (1.0 - lower_bounds)[:, None],
                               zeros, zeros], axis=1)
    vecs = jnp.stack([ret_gn_w, hg_norm_w, gla_norm_w, s5_d, s5_glu_b], axis=1)
    vecs = jnp.concatenate([vecs, zeros], axis=1)
    wa_pad = jnp.stack([jnp.pad(gla_wa2[:, dr], ((0, 0), (GLA_RANK * dr, 128 - GLA_RANK * (dr + 1)), (0, 0)))
                        for dr in range(2)], axis=1)
    w_out16, glu16 = w_out.astype(BF16), s5_glu_w.astype(BF16)

    for l in range(depth):
        p_ret, p_s5, p_hg, p_gla = _project(xs, mods[l], norm1_w[l], w_in_pad[l], n_ctx_tiles)

        yrf = _ret_scan(p_ret, cos, sin, n_ctx_tiles, False)
        yrb = _ret_scan(p_ret, cos, sin, n_ctx_tiles, True)
        ys5 = _s5_mix(p_s5, *(o[l] for o in s5_ops), ctx_len)
        yhf = _hg_scan(p_hg, lb_rows[l], n_ctx_tiles, False)
        yhb = _hg_scan(p_hg, lb_rows[l], n_ctx_tiles, True)
        ygs = [_gla_scan(p_gla, wa_pad[l, dr], gla_ba[l, dr][None, :], n_ctx_tiles, bool(dr)) for dr in range(2)]

        j = l // 2
        router = (router_w[j], router_b[j]) if l % 2 == 1 else None
        outs = _mix_out(xs, mods[l], p_ret, yrf, yrb, p_s5, ys5, p_hg, yhf, yhb, p_gla, ygs[0], ygs[1],
                        vecs[l], glu16[l], w_out16[l], norm2_w[l], router, n_ctx_tiles)
        if l % 2 == 0:
            x_new, tok = outs
            xs = _dense_ffn(x_new, tok, mods[l], ffn_w1[j], ffn_w3[j], ffn_w2[j], n_ctx_tiles)
        else:
            x_new, tok, logits = outs
            n_tok = bsz * xs.shape[1]
            dest, gates, row_tok, block_e = _routing(logits.reshape(n_tok, 128))
            ybuf = _moe_experts(tok.reshape(n_tok, d), row_tok, block_e, moe_w1[j], moe_w3[j], moe_w2[j])
            xs = _moe_combine(x_new, mods[l], ybuf, dest, gates, n_ctx_tiles)

    return _final_norm(xs, final_norm_w, n_ctx_tiles)
```

```python
import functools
import math

import numpy as np
import jax
import jax.numpy as jnp
from jax import lax
from jax.experimental import pallas as pl
from jax.experimental.pallas import tpu as pltpu

F32 = jnp.float32
BF16 = jnp.bfloat16

D_MODEL = 1024
GRID_W = 64
MIX_W = 256
NORM_EPS = 1e-6
ROPE_BASE = 10000.0
LB_FLOOR = 1e-30
LB_CEIL = 1.0 - 1e-6
N_HEADS = 4
S5_GROUP = 16
S5_GROUPS = 16
S5_STATE = 64
S5_SUB = 16
GLA_RANK = 16
GLA_TAU = 16.0
N_EXPERTS = 8
TOP_K = 2
D_IN = 3360
D_IN_PAD = 3456

SCAN_CHUNK = 256
SCAN_SUB = 16
TAME_LIMIT = 40.0
NEG_BIG = -1e30
VMEM_LIMIT = 56 * 1024 * 1024


def _cparams(sem):
    return pltpu.CompilerParams(dimension_semantics=sem, vmem_limit_bytes=VMEM_LIMIT)


def _full(shape):
    n = len(shape)
    return pl.BlockSpec(shape, lambda *_: (0,) * n)


def _dot(a, b):
    return jnp.dot(a, b, preferred_element_type=F32)


def _dot_nt(a, b):
    return lax.dot_general(a, b, (((1,), (1,)), ((), ())), preferred_element_type=F32)


def _dot_hi(a, b):
    return jnp.dot(a, b, preferred_element_type=F32, precision=lax.Precision.HIGHEST)


def _sigmoid(z):
    return 0.5 * (jnp.tanh(0.5 * z) + 1.0)


def _silu(z):
    return z * _sigmoid(z)


def _log_sigmoid(z):
    return jnp.minimum(z, 0.0) - jnp.log(1.0 + jnp.exp(-jnp.abs(z)))


def _scan_constants(n_heads, wk, wv, reverse):
    c, s, hc = SCAN_CHUNK, SCAN_SUB, SCAN_CHUNK // 2
    i = np.arange(c)[:, None]
    r = np.arange(c)[None, :]
    cum = (r >= i) if reverse else (r <= i)
    ih = np.arange(hc)[:, None]
    jh = np.arange(hc)[None, :]
    causal = (jh >= ih) if reverse else (jh <= ih)
    masks = [((ih // s) == (jh // s)) & causal]
    m = s
    while m < hc:
        same = (ih // (2 * m)) == (jh // (2 * m))
        i_hi, j_hi = (ih % (2 * m)) >= m, (jh % (2 * m)) >= m
        masks.append(same & (~i_hi & j_hi if reverse else i_hi & ~j_hi))
        m *= 2
    dk, dv = wk // n_heads, wv // n_heads
    lk, lv = np.arange(wk), np.arange(wv)
    hmv = (lv[None, :] // dv) == np.arange(n_heads)[:, None]
    bd = (lv[:, None] // dv) == (lk[None, :] // dk)
    return dict(
        cm=jnp.asarray(np.concatenate([cum] * 2, axis=1), BF16),
        pm=jnp.asarray(np.stack([np.tile(x, (1, n_heads)) for x in masks]), F32),
        hmv=jnp.asarray(hmv[:, None, :], BF16),
        bd=jnp.asarray(bd, F32),
        gmat=jnp.asarray(bd.T, BF16),
    )


def _split3(g):
    hi = g.astype(BF16)
    r1 = g - hi.astype(F32)
    mid = r1.astype(BF16)
    lo = (r1 - mid.astype(F32)).astype(BF16)
    return hi, mid, lo


def _head_scores(qd, k16, n_heads):
    r, wk = qd.shape
    dk = wk // n_heads
    per_col = 128 // dk
    lane_head = lax.broadcasted_iota(jnp.int32, (1, 128), 1) // dk
    blocks = []
    for col in range(wk // 128):
        qc = qd[:, col * 128:(col + 1) * 128]
        lhs = jnp.concatenate([jnp.where(lane_head == h, qc, 0.0) for h in range(per_col)], axis=0)
        sc = _dot_nt(lhs.astype(BF16), k16[:, col * 128:(col + 1) * 128])
        blocks += [sc[h * r:(h + 1) * r, :] for h in range(per_col)]
    return jnp.concatenate(blocks, axis=1)


def _ref_rows(bb_ref, rows, span):
    wk = bb_ref.shape[1]
    pieces = [jnp.zeros((span, wk), F32) if r is None else jnp.broadcast_to(bb_ref[r:r + 1, :], (span, wk))
              for r in rows]
    return pieces[0] if len(pieces) == 1 else jnp.concatenate(pieces, axis=0)


def _scan_core(q, k, v, g, cm_ref, pm_ref, hmv_ref, bd_ref, st_ref, bb_ref, o_ref, *, reverse):
    c, s, hc = SCAN_CHUNK, SCAN_SUB, SCAN_CHUNK // 2
    wk, wv = q.shape[1], v.shape[1]
    n_heads = hmv_ref.shape[0]
    g_hi = g.astype(BF16)
    g_lo = (g - g_hi.astype(F32)).astype(BF16)
    b = _dot(cm_ref[...], jnp.concatenate([g_hi, g_lo], axis=0))
    bb_ref[...] = b
    end = 0 if reverse else c - 1
    b_end = bb_ref[end:end + 1, :]
    st = st_ref[...]
    o_inter = _dot_nt((q * jnp.exp(b)).astype(BF16), st.astype(BF16))
    upd = _dot(v.T.astype(BF16), (k * jnp.exp(b_end - b)).astype(BF16))
    st_ref[...] = st * jnp.exp(b_end) + upd * bd_ref[...]

    v16 = v.astype(BF16)
    vexp = [_expand_heads(v16[h0:h0 + hc], hmv_ref[...]) for h0 in (0, hc)]
    halves = (slice(0, hc), slice(hc, c))

    n_blk = c // s
    if reverse:
        rows = [(i + 1) * s for i in range(n_blk - 1)] + [None]
    else:
        rows = [None] + [i * s - 1 for i in range(1, n_blk)]
    dq = b - _ref_rows(bb_ref, rows, s)
    low = jnp.min(jnp.min(dq, axis=1, keepdims=True), axis=0, keepdims=True)
    qd = q * jnp.exp(dq)
    k16 = (k * jnp.exp(jnp.minimum(-dq, TAME_LIMIT))).astype(BF16)
    pm_diag = pm_ref[0] * (low >= -TAME_LIMIT).astype(F32)
    a = [_head_scores(qd[h], k16[h], n_heads) * pm_diag for h in halves]

    m, lev = s, 1
    while m < hc:
        n_pair = c // (2 * m)
        rows = [p * 2 * m + (m if reverse else m - 1) for p in range(n_pair)]
        ed = jnp.exp(-jnp.abs(b - _ref_rows(bb_ref, rows, 2 * m)))
        qd, k16 = q * ed, (k * ed).astype(BF16)
        a = [a[i] + _head_scores(qd[h], k16[h], n_heads) * pm_ref[lev] for i, h in enumerate(halves)]
        m, lev = 2 * m, lev + 1

    ed = jnp.exp(-jnp.abs(b - _ref_rows(bb_ref, [hc if reverse else hc - 1], c)))
    qd, k16 = q * ed, (k * ed).astype(BF16)
    late, early = (0, 1) if reverse else (1, 0)
    top = _head_scores(qd[halves[late]], k16[halves[early]], n_heads)

    o_ref[halves[early], :] = o_inter[halves[early]] + _dot(a[early].astype(BF16), vexp[early])
    a_late = jnp.concatenate([a[late], top], axis=1).astype(BF16)
    o_ref[halves[late], :] = o_inter[halves[late]] + _dot(a_late, jnp.concatenate([vexp[late], vexp[early]], axis=0))

    return low[0, 0] < -TAME_LIMIT


def _explicit_diagonal(q, k, v, gmat_ref, bb_ref, qb_ref, kb_ref, vb_ref, wj_ref, o_ref, reverse):
    c, s = SCAN_CHUNK, SCAN_SUB
    wv = v.shape[1]
    qb_ref[...] = q
    kb_ref[...] = k
    vb_ref[...] = v
    rows = lax.broadcasted_iota(jnp.int32, (s, 1), 0)

    def blk(ib, carry):
        base = pl.multiple_of(ib * s, s)
        bblk = bb_ref[pl.ds(base, s), :]
        qblk = qb_ref[pl.ds(base, s), :]
        for jo in range(s):
            bj = bb_ref[pl.ds(base + jo, 1), :]
            kj = kb_ref[pl.ds(base + jo, 1), :]
            valid = (rows <= jo) if reverse else (rows >= jo)
            w = qblk * kj * jnp.exp(jnp.where(valid, bblk - bj, NEG_BIG))
            wj_ref[pl.ds(jo * s, s), :] = w.astype(BF16)
        r = _dot(wj_ref[...], gmat_ref[...])
        acc = jnp.zeros((s, wv), F32)
        for jo in range(s):
            acc = acc + r[jo * s:(jo + 1) * s, :] * vb_ref[pl.ds(base + jo, 1), :]
        o_ref[pl.ds(base, s), :] += acc
        return carry

    lax.fori_loop(0, c // s, blk, 0)


def _rope(t, cos, sin_signed):
    outs = []
    for h in range(t.shape[1] // 128):
        th = t[:, h * 128:(h + 1) * 128]
        lane = lax.broadcasted_iota(jnp.int32, th.shape, 1)
        partner = jnp.where((lane % 32) < 16, pltpu.roll(th, 128 - 16, 1), pltpu.roll(th, 16, 1))
        outs.append(th * cos[:, h * 128:(h + 1) * 128] + partner * sin_signed[:, h * 128:(h + 1) * 128])
    return jnp.concatenate(outs, axis=1)


def _expand_heads(x16, hm):
    return (x16[None] * hm).reshape(hm.shape[0] * x16.shape[0], x16.shape[1])


def _ret_constants(reverse):
    c, hc = SCAN_CHUNK, SCAN_CHUNK // 2
    lg = jnp.log1p(-(2.0 ** (-5.0 - jnp.arange(N_HEADS, dtype=F32))))
    lg = lg[::-1] if reverse else lg
    lane = jnp.repeat(lg, MIX_W // N_HEADS)[None, :]
    i = jnp.arange(c, dtype=F32)[:, None]
    steps_in = (c - i) if reverse else (i + 1.0)
    steps_out = i if reverse else (c - 1.0 - i)
    dec = jnp.concatenate([jnp.exp(lane * steps_in), jnp.exp(lane * steps_out),
                           jnp.broadcast_to(jnp.exp(lane * c), (8, MIX_W))], axis=0)
    ih = jnp.arange(hc, dtype=F32)[:, None]
    jh = jnp.arange(hc, dtype=F32)[None, :]
    dist = (jh - ih) if reverse else (ih - jh)
    per_head = lambda d: jnp.concatenate(
        [jnp.where(d >= 0, jnp.exp(lg[h] * jnp.maximum(d, 0.0)), 0.0) for h in range(N_HEADS)], axis=1)
    dm = jnp.stack([per_head(dist), per_head(dist + hc)])
    hd = np.arange(MIX_W) // (MIX_W // N_HEADS)
    hm = jnp.asarray((hd[None, :] == np.arange(N_HEADS)[:, None])[:, None, :], BF16)
    bd = jnp.asarray(hd[:, None] == hd[None, :], F32)
    return dec, dm, hm, bd


SCAN_BATCH = 4


def _ret_kernel(p_ref, cos_ref, sin_ref, dec_ref, dm_ref, hm_ref, bd_ref, o_ref, st_ref, *, reverse):
    @pl.when(pl.program_id(1) == 0)
    def _():
        st_ref[...] = jnp.zeros_like(st_ref)

    c, hc = SCAN_CHUNK, SCAN_CHUNK // 2
    cos, sin = cos_ref[...], sin_ref[...]
    hm = hm_ref[...]
    halves = (slice(0, hc), slice(hc, c))
    late, early = (0, 1) if reverse else (1, 0)
    for bi in range(p_ref.shape[0]):
        q = _rope(p_ref[bi, :, 0:256], cos, sin)
        k = _rope(p_ref[bi, :, 256:512], cos, sin) * (64.0 ** -0.5)
        v = p_ref[bi, :, 512:768]
        st = st_ref[bi]
        o_inter = _dot_nt((q * dec_ref[0:c, :]).astype(BF16), st.astype(BF16))
        upd = _dot(v.T.astype(BF16), (k * dec_ref[c:2 * c, :]).astype(BF16))
        st_ref[bi] = st * dec_ref[2 * c:2 * c + 1, :] + upd * bd_ref[...]

        k16, v16 = k.astype(BF16), v.astype(BF16)
        vx = [_expand_heads(v16[h], hm) for h in halves]
        a_early = _head_scores(q[halves[early]], k16[halves[early]], N_HEADS) * dm_ref[0]
        a_late = _head_scores(q[halves[late]], k16[halves[late]], N_HEADS) * dm_ref[0]
        top = _head_scores(q[halves[late]], k16[halves[early]], N_HEADS) * dm_ref[1]
        o_ref[bi, halves[early], :] = o_inter[halves[early]] + _dot(a_early.astype(BF16), vx[early])
        a_late = jnp.concatenate([a_late, top], axis=1).astype(BF16)
        o_ref[bi, halves[late], :] = (o_inter[halves[late]]
                                      + _dot(a_late, jnp.concatenate([vx[late], vx[early]], axis=0)))


def _hg_inputs(p_ref, extra, reverse):
    (lb_ref,) = extra
    q = p_ref[:, 0:256]
    z = p_ref[:, 512:768] if reverse else p_ref[:, 256:512]
    v = p_ref[:, 768:1024]
    lb_floor, one_m_lb = lb_ref[0:1, :], lb_ref[1:2, :]
    sg = _sigmoid(z)
    return q, one_m_lb * (1.0 - sg), v, jnp.log(lb_floor + one_m_lb * sg)


def _gla_inputs(p_ref, extra, reverse):
    wa_ref, ba_ref = extra
    q = p_ref[:, 0:128]
    k = p_ref[:, 128:256] * (32.0 ** -0.5)
    v = p_ref[:, 256:512]
    g = _log_sigmoid(_dot_hi(p_ref[:, 768:896], wa_ref[...]) + ba_ref[...]) * (1.0 / GLA_TAU)
    return q, k, v, g


def _scan_kernel(p_ref, *rest, inputs_fn, n_extra, reverse):
    extra, rest = rest[:n_extra], rest[n_extra:]
    (cm_ref, pm_ref, hmv_ref, bd_ref, gmat_ref, o_ref, st_ref, bb_ref, *fallback_scratch) = rest

    @pl.when(pl.program_id(1) == 0)
    def _():
        st_ref[...] = jnp.zeros_like(st_ref)

    n_b = p_ref.shape[0]
    untame = []
    for bi in range(n_b):
        q, k, v, g = inputs_fn(p_ref.at[bi], extra, reverse)
        untame.append(_scan_core(q, k, v, g, cm_ref, pm_ref, hmv_ref, bd_ref,
                                 st_ref.at[bi], bb_ref.at[bi], o_ref.at[bi], reverse=reverse))
    for bi in range(n_b):
        @pl.when(untame[bi])
        def _(bi=bi):
            q, k, v, _ = inputs_fn(p_ref.at[bi], extra, reverse)
            _explicit_diagonal(q, k, v, gmat_ref, bb_ref.at[bi], *fallback_scratch, o_ref.at[bi], reverse)


def _scan_tile_map(n_tiles, n_ctx_tiles, reverse):
    if not reverse:
        return lambda step: step
    return lambda step: jnp.where(step < n_ctx_tiles, n_ctx_tiles - 1 - step, n_tiles - 1 - (step - n_ctx_tiles))


def _scan_call(inputs_fn, p, extra, wk, wv, n_ctx_tiles, reverse):
    bsz, t_all, wp = p.shape
    c, s, nb = SCAN_CHUNK, SCAN_SUB, math.gcd(SCAN_BATCH, bsz)
    n_tiles = t_all // c
    cst = _scan_constants(N_HEADS, wk, wv, reverse)
    tile = _scan_tile_map(n_tiles, n_ctx_tiles, reverse)
    consts = [cst[n] for n in ('cm', 'pm', 'hmv', 'bd', 'gmat')]
    in_specs = [pl.BlockSpec((nb, c, wp), lambda b, t: (b, tile(t), 0))]
    in_specs += [_full(x.shape) for x in extra] + [_full(x.shape) for x in consts]
    return pl.pallas_call(
        functools.partial(_scan_kernel, inputs_fn=inputs_fn, n_extra=len(extra), reverse=reverse),
        grid=(bsz // nb, n_tiles),
        in_specs=in_specs,
        out_specs=pl.BlockSpec((nb, c, wv), lambda b, t: (b, tile(t), 0)),
        out_shape=jax.ShapeDtypeStruct((bsz, t_all, wv), F32),
        scratch_shapes=[
            pltpu.VMEM((nb, wv, wk), F32), pltpu.VMEM((nb, c, wk), F32),
            pltpu.VMEM((c, wk), F32), pltpu.VMEM((c, wk), F32), pltpu.VMEM((c, wv), F32),
            pltpu.VMEM((s * s, wk), BF16),
        ],
        compiler_params=_cparams(("arbitrary", "arbitrary")),
    )(p, *extra, *consts)


def _ret_scan(p_ret, cos, sin, n_ctx_tiles, reverse):
    bsz, t_all, wp = p_ret.shape
    c, nb = SCAN_CHUNK, math.gcd(SCAN_BATCH, bsz)
    tile = _scan_tile_map(t_all // c, n_ctx_tiles, reverse)
    consts = _ret_constants(reverse)
    rows = lambda w: pl.BlockSpec((c, w), lambda b, t: (tile(t), 0))
    return pl.pallas_call(
        functools.partial(_ret_kernel, reverse=reverse),
        grid=(bsz // nb, t_all // c),
        in_specs=[pl.BlockSpec((nb, c, wp), lambda b, t: (b, tile(t), 0)), rows(MIX_W), rows(MIX_W)]
        + [_full(x.shape) for x in consts],
        out_specs=pl.BlockSpec((nb, c, MIX_W), lambda b, t: (b, tile(t), 0)),
        out_shape=jax.ShapeDtypeStruct((bsz, t_all, MIX_W), F32),
        scratch_shapes=[pltpu.VMEM((nb, MIX_W, MIX_W), F32)],
        compiler_params=_cparams(("arbitrary", "arbitrary")),
    )(p_ret, cos, sin, *consts)


def _hg_scan(p_hg, lb_rows, n_ctx_tiles, reverse):
    return _scan_call(_hg_inputs, p_hg, [lb_rows], 256, 256, n_ctx_tiles, reverse)


def _gla_scan(p_gla, wa, ba, n_ctx_tiles, reverse):
    return _scan_call(_gla_inputs, p_gla, [wa, ba], 128, 256, n_ctx_tiles, reverse)


def _rope_tables(seq, ctx_len):
    t_idx = jnp.arange(seq, dtype=jnp.int32)
    row = (t_idx // GRID_W).astype(F32)
    col = (t_idx % GRID_W).astype(F32)
    freqs = ROPE_BASE ** (-jnp.arange(16, dtype=F32) / 16)
    ang_r = row[:, None] * freqs[None, :]
    ang_c = col[:, None] * freqs[None, :]
    cos = jnp.concatenate([jnp.cos(ang_r)] * 2 + [jnp.cos(ang_c)] * 2, axis=1)
    sin = jnp.concatenate([-jnp.sin(ang_r), jnp.sin(ang_r), -jnp.sin(ang_c), jnp.sin(ang_c)], axis=1)
    cos = jnp.concatenate([jnp.ones((ctx_len, 64), F32), cos], axis=0)
    sin = jnp.concatenate([jnp.zeros((ctx_len, 64), F32), sin], axis=0)
    return jnp.tile(cos, (1, N_HEADS)), jnp.tile(sin, (1, N_HEADS))


def _s5_response_kernel(k_ref, rep_ref, o_ref):
    n = S5_SUB
    size = n * S5_GROUP
    diff = (lax.broadcasted_iota(jnp.int32, (size, size), 1) // S5_GROUP
            - lax.broadcasted_iota(jnp.int32, (size, size), 0) // S5_GROUP)

    def tiled(block):
        return jnp.concatenate([_dot_hi(block, rep_ref[...])] * n, axis=0)

    m = jnp.where(diff == 0, tiled(k_ref[0, 0] + k_ref[1, 0]), 0.0)
    for tau in range(1, n):
        m = jnp.where(diff == tau, tiled(k_ref[0, tau]), m)
        m = jnp.where(diff == -tau, tiled(k_ref[1, tau]), m)
    o_ref[...] = m.astype(o_ref.dtype)


def _s5_response(kern):
    depth, _, g, n, cg, _ = kern.shape
    rep = jnp.asarray(np.tile(np.eye(cg), (1, n)), F32)
    return pl.pallas_call(
        _s5_response_kernel, grid=(depth, g),
        in_specs=[pl.BlockSpec((None, 2, None, n, cg, cg), lambda l, gi: (l, 0, gi, 0, 0, 0)), _full(rep.shape)],
        out_specs=pl.BlockSpec((None, None, n * cg, n * cg), lambda l, gi: (l, gi, 0, 0)),
        out_shape=jax.ShapeDtypeStruct((depth, g, n * cg, n * cg), BF16),
        compiler_params=_cparams(("arbitrary", "arbitrary")),
    )(kern, rep)


def _s5_operators(lam_re, lam_im, log_dt, b_re, b_im, c_re, c_im):
    n = S5_SUB
    lam = lax.complex(lam_re, lam_im)
    log_a = lam * jnp.exp(log_dt)[..., None]
    a_bar = jnp.exp(log_a)
    b_bar = ((a_bar - 1) / lam)[..., None] * lax.complex(b_re, b_im)
    c_mat = lax.complex(c_re, c_im)
    step = np.arange(n)
    pw = jnp.exp(log_a[..., None, :] * jnp.arange(n + 1, dtype=F32)[:, None])
    kern = jnp.einsum('...cp,...tp,...pd->...tdc', c_mat, pw[..., :n, :], b_bar).real
    m = _s5_response(kern)
    qc =pw[..., n - 1 - step, :][..., None] * b_bar[..., None, :, :]
    qc = jnp.swapaxes(qc, -1, -2)
    z = c_mat[..., None, :, :] * pw[..., step + 1, :][..., None, :]
    z = jnp.moveaxis(z, -1, -3)
    qc = jnp.stack([qc[:, 0], jnp.flip(qc[:, 1], axis=-3)], axis=1)
    z = jnp.stack([z[:, 0], jnp.flip(z[:, 1], axis=-2)], axis=1)
    lead = qc.shape[:3]
    qc =qc.reshape(*lead, n * S5_GROUP, S5_STATE)
    z = z.reshape(*lead, S5_STATE, n * S5_GROUP)
    q = jnp.concatenate([qc.real, qc.imag, qc.imag, qc.real], axis=-1)
    pm = jnp.concatenate([z.real, -z.imag, jnp.zeros_like(z.real), jnp.zeros_like(z.real)], axis=-2)
    a_row = pw[..., n, :]
    a1 = jnp.concatenate([a_row.real] * 4, axis=-1)
    a2 = jnp.concatenate([-a_row.imag, a_row.imag, a_row.imag, -a_row.imag], axis=-1)
    a = jnp.stack([a1[:, 0], a2[:, 0], a1[:, 1], a2[:, 1]], axis=1)
    return (m, jnp.concatenate([q[:, 0], q[:, 1]], axis=-1).astype(BF16),
            pm[:, 0].astype(BF16), pm[:, 1].astype(BF16), a)


S5_GROUP_BLOCK = 8


S5_TILE = S5_SUB * S5_SUB


def _s5_in_kernel(u_ref, q_ref, wf_ref, wb_ref, uf_ref):
    gl = S5_GROUP_BLOCK * S5_GROUP

    def regroup(t, carry):
        x3 = u_ref[pl.ds(pl.multiple_of(t * S5_TILE, S5_TILE), S5_TILE), :].astype(BF16).reshape(S5_SUB, S5_SUB, gl)
        for k in range(S5_GROUP_BLOCK):
            piece = x3[:, :, k * S5_GROUP:(k + 1) * S5_GROUP].reshape(S5_SUB, S5_SUB * S5_GROUP)
            uf_ref[k, pl.ds(pl.multiple_of(t * S5_SUB, S5_SUB), S5_SUB), :] = piece
        return carry

    lax.fori_loop(0, u_ref.shape[0] // S5_TILE, regroup, 0)
    for k in range(S5_GROUP_BLOCK):
        w = _dot(uf_ref[k], q_ref[k])
        wf_ref[:, k, :] = w[:, :256]
        wb_ref[:, k, :] = w[:, 256:]


def _s5_state_kernel(wf_ref, wb_ref, a_ref, hf_ref, hb_ref, sf_ref, sb_ref):
    @pl.when(pl.program_id(0) == 0)
    def _():
        sf_ref[...] = jnp.zeros_like(sf_ref)
        sb_ref[...] = jnp.zeros_like(sb_ref)

    def swap(h):
        return jnp.concatenate([h[:, 128:], h[:, :128]], axis=1)

    n = wf_ref.shape[0]
    hf, hb = sf_ref[...], sb_ref[...]
    a1f, a2f, a1b, a2b = a_ref[0], a_ref[1], a_ref[2], a_ref[3]
    for i in range(n):
        hf_ref[i] = hf
        hf = a1f * hf + a2f * swap(hf) + wf_ref[i]
        j = n - 1 - i
        hb_ref[j] = hb
        hb = a1b * hb + a2b * swap(hb) + wb_ref[j]
    sf_ref[...] = hf
    sb_ref[...] = hb


def _s5_out_kernel(u_ref, m_ref, hf_ref, hb_ref, pf_ref, pb_ref, y_ref, ys_ref, hs_ref):
    for k in range(S5_GROUP_BLOCK):
        hs_ref[k % 2, 0] = hf_ref[:, k, :]
        hs_ref[k % 2, 1] = hb_ref[:, k, :]
        y = _dot(u_ref[k], m_ref[k])
        y = y + _dot(hs_ref[k % 2, 0].astype(BF16), pf_ref[k])
        y = y + _dot(hs_ref[k % 2, 1].astype(BF16), pb_ref[k])
        ys_ref[k] = y

    def regroup(t, carry):
        rows = pl.ds(pl.multiple_of(t * S5_SUB, S5_SUB), S5_SUB)
        pieces = [ys_ref[k, rows, :].reshape(S5_SUB, S5_SUB, S5_GROUP) for k in range(S5_GROUP_BLOCK)]
        y_ref[pl.ds(pl.multiple_of(t * S5_TILE, S5_TILE), S5_TILE), :] = (
            jnp.concatenate(pieces, axis=-1).reshape(S5_TILE, S5_GROUP_BLOCK * S5_GROUP))
        return carry

    lax.fori_loop(0, y_ref.shape[0] // S5_TILE, regroup, 0)


def _s5_mix(u, m, q, pf, pb, a_rows, ctx_len):
    bsz, t_all, _ = u.shape
    n, g = S5_SUB, S5_GROUPS
    n_rows = t_all // n
    blk = 16
    n_steps = n_rows // blk
    n_ctx = (ctx_len // n) // blk
    gb = S5_GROUP_BLOCK
    n_gb = g // gb
    nat_spec = pl.BlockSpec((None, t_all, gb * S5_GROUP), lambda b, gi: (b, 0, gi))
    u_spec = pl.BlockSpec((None, gb, n_rows, 256), lambda b, gi: (b, gi, 0, 0))
    st_spec = pl.BlockSpec((n_rows, gb, 256), lambda b, gi: (0, b * n_gb + gi, 0))
    hshape = jax.ShapeDtypeStruct((n_rows, bsz * g, 256), F32)
    wf, wb, uf = pl.pallas_call(
        _s5_in_kernel, grid=(bsz, n_gb),
        in_specs=[nat_spec, pl.BlockSpec((gb, 256, 512), lambda b, gi: (gi, 0, 0))],
        out_specs=[st_spec, st_spec, u_spec],
        out_shape=[hshape, hshape, jax.ShapeDtypeStruct((bsz, g, n_rows, 256), BF16)],
        compiler_params=_cparams(("arbitrary", "arbitrary")),
    )(u, q)

    a = jnp.tile(a_rows, (1, bsz, 1))
    fwd = lambda s: (s, 0, 0)
    bwd = lambda s: (jnp.where(s < n_ctx, n_ctx - 1 - s, n_steps - 1 - (s - n_ctx)), 0, 0)
    st_block = (blk, bsz * g, 256)
    hf, hb = pl.pallas_call(
        _s5_state_kernel, grid=(n_steps,),
        in_specs=[pl.BlockSpec(st_block, fwd), pl.BlockSpec(st_block, bwd), _full(a.shape)],
        out_specs=[pl.BlockSpec(st_block, fwd), pl.BlockSpec(st_block, bwd)],
        out_shape=[hshape, hshape],
        scratch_shapes=[pltpu.VMEM((bsz * g, 256), F32), pltpu.VMEM((bsz * g, 256), F32)],
        compiler_params=_cparams(("arbitrary",)),
    )(wf, wb, a)

    mat_spec = pl.BlockSpec((gb, 256, 256), lambda b, gi: (gi, 0, 0))
    return pl.pallas_call(
        _s5_out_kernel, grid=(bsz, n_gb),
        in_specs=[u_spec, mat_spec, st_spec, st_spec, mat_spec, mat_spec],
        out_specs=nat_spec,
        out_shape=jax.ShapeDtypeStruct((bsz, t_all, g * S5_GROUP), F32),
        scratch_shapes=[pltpu.VMEM((gb, n_rows, 256), F32), pltpu.VMEM((2, 2, n_rows, 256), F32)],
        compiler_params=_cparams(("arbitrary", "arbitrary")),
    )(uf, m, hf, hb, pf, pb)


ROW_TILE = 256
M_SHIFT1, M_SCALE1, M_GATE1, M_SHIFT2, M_SCALE2, M_GATE2 = range(6)


def _ada_kernel(cond_ref, w_ref, b_ref, o_ref):
    o_ref[...] = _dot_hi(_silu(cond_ref[...]), w_ref[...]) + b_ref[...]


def _modulation(c, c_ctx, ada_w, ada_b):
    depth, d, d6 = ada_w.shape
    bsz = c.shape[0]
    cond = jnp.zeros((8, d), F32).at[:bsz].set(c).at[bsz].set(c_ctx)
    out = pl.pallas_call(
        _ada_kernel, grid=(depth, d6 // d),
        in_specs=[_full((8, d)),
                  pl.BlockSpec((None, d, d), lambda l, j: (l, 0, j)),
                  pl.BlockSpec((None, 1, d), lambda l, j: (l, 0, j))],
        out_specs=pl.BlockSpec((None, 8, d), lambda l, j: (l, 0, j)),
        out_shape=jax.ShapeDtypeStruct((depth, 8, d6), F32),
        compiler_params=_cparams(("arbitrary", "arbitrary")),
    )(cond, ada_w, ada_b[:, None, :])
    out = out.reshape(depth, 8, 6, d)
    lat = out[:, :bsz]
    ctx = jnp.broadcast_to(out[:, bsz][:, None], lat.shape)
    mods = jnp.stack([ctx, lat], axis=2)
    return jnp.pad(mods, ((0, 0), (0, 0), (0, 0), (0, 2), (0, 0)))


def _rms(x, w):
    return x * lax.rsqrt(jnp.mean(x * x, axis=-1, keepdims=True) + NORM_EPS) * w


def _proj_kernel(x_ref, mod_ref, nw_ref, w_ref, ret_ref, s5_ref, hg_ref, gla_ref):
    h = _rms(x_ref[...], nw_ref[...]) * (1.0 + mod_ref[M_SCALE1:M_SCALE1 + 1, :]) + mod_ref[M_SHIFT1:M_SHIFT1 + 1, :]
    p = _dot(h.astype(BF16), w_ref[...])
    ret_ref[...] = p[:, 0:1024]
    s5_ref[...] = p[:, 1024:1280]
    hg_ref[...] = p[:, 1280:2560]
    gla_ref[...] = p[:, 2560:3456]


def _tile_specs(n_ctx_tiles, nb=None):
    tm = ROW_TILE
    row = lambda w, j=0: pl.BlockSpec((nb, tm, w), lambda b, t: (b, t, j))
    mod = pl.BlockSpec((nb, None, 8, D_MODEL), lambda b, t: (b, jnp.where(t >= n_ctx_tiles, 1, 0), 0, 0))
    return row, mod


def _per_batch(fn, batched):
    def wrapped(*refs, **kw):
        n_b = next(r.shape[0] for r, flag in zip(refs, batched) if flag)
        for bi in range(n_b):
            fn(*[r.at[bi] if flag else r for r, flag in zip(refs, batched)], **kw)
    return wrapped


def _project(xs, mods, norm_w, w_pad, n_ctx_tiles):
    bsz, t_all, d = xs.shape
    nb = math.gcd(SCAN_BATCH, bsz)
    row, mod = _tile_specs(n_ctx_tiles, nb)
    widths = (1024, 256, 1280, 896)
    return pl.pallas_call(
        _per_batch(_proj_kernel, (True, True, False, False) + (True,) * len(widths)),
        grid=(bsz // nb, t_all // ROW_TILE),
        in_specs=[row(d), mod, _full((1, d)), _full(w_pad.shape)],
        out_specs=[row(w) for w in widths],
        out_shape=[jax.ShapeDtypeStruct((bsz, t_all, w), F32) for w in widths],
        compiler_params=_cparams(("arbitrary", "arbitrary")),
    )(xs, mods, norm_w[None, :], w_pad)


def _dot3(x, mat):
    hi, mid, lo = _split3(x)
    return _dot(hi, mat) + _dot(mid, mat) + _dot(lo, mat)


def _head_norm(y, gavg, w, center):
    if center:
        y = y - _dot3(y, gavg)
    return y * lax.rsqrt(_dot3(y * y, gavg) + NORM_EPS) * w


def _gelu_tanh(x):
    return 0.5 * x * (1.0 + jnp.tanh(math.sqrt(2.0 / math.pi) * (x + 0.044715 * (x * x * x))))


def _out_kernel(x_ref, mod_ref, gret_ref, yrf_ref, yrb_ref, u_ref, ys5_ref, ghg_ref, yhf_ref, yhb_ref,
                ggl_ref, ygf_ref, ygb_ref, vec_ref, glu_ref, gavg_ref, wo_ref, nw_ref, *rest, with_router):
    if with_router:
        rw_ref, rb_ref, xo_ref, tok_ref, lg_ref = rest
    else:
        xo_ref, tok_ref = rest
    gavg = gavg_ref[...]
    y_ret = _head_norm(yrf_ref[...] + yrb_ref[...], gavg, vec_ref[0:1, :], True) * _silu(gret_ref[...])
    u = u_ref[...]
    z = _gelu_tanh(ys5_ref[...] + vec_ref[3:4, :] * u)
    y_s5 = z * _sigmoid(_dot(z.astype(BF16), glu_ref[...]) + vec_ref[4:5, :])
    y_hg = _head_norm(yhf_ref[...] + yhb_ref[...], gavg, vec_ref[1:2, :], False) * _silu(ghg_ref[...])
    y_gla = _head_norm(ygf_ref[...] + ygb_ref[...], gavg, vec_ref[2:3, :], False) * _silu(ggl_ref[...])
    acc = _dot(y_ret.astype(BF16), wo_ref[0:256, :])
    acc = acc + _dot(y_s5.astype(BF16), wo_ref[256:512, :])
    acc = acc + _dot(y_hg.astype(BF16), wo_ref[512:768, :])
    acc = acc + _dot(y_gla.astype(BF16), wo_ref[768:1024, :])
    x_new = x_ref[...] + mod_ref[M_GATE1:M_GATE1 + 1, :] * acc
    xo_ref[...] = x_new
    tok = _rms(x_new, nw_ref[...]) * (1.0 + mod_ref[M_SCALE2:M_SCALE2 + 1, :]) + mod_ref[M_SHIFT2:M_SHIFT2 + 1, :]
    tok_ref[...] = tok
    if with_router:
        t_hi = tok.astype(BF16)
        t_lo = (tok - t_hi.astype(F32)).astype(BF16)
        lg_ref[...] = (_dot(t_hi, rw_ref[0]) + _dot(t_lo, rw_ref[0]) + _dot(t_hi, rw_ref[1])) + rb_ref[...]


def _mix_out(xs, mods, p_ret, yrf, yrb, p_s5, ys5, p_hg, yhf, yhb, p_gla, ygf, ygb,
             vec, glu_w, w_out, norm2_w, router, n_ctx_tiles):
    bsz, t_all, d = xs.shape
    nb = math.gcd(SCAN_BATCH, bsz)
    row, mod = _tile_specs(n_ctx_tiles, nb)
    head = np.arange(MIX_W) // (MIX_W // N_HEADS)
    gavg = jnp.asarray((head[:, None] == head[None, :]) / (MIX_W // N_HEADS), BF16)
    ins = [xs, mods, p_ret, yrf, yrb, p_s5, ys5, p_hg, yhf, yhb, p_gla, ygf, ygb,
           vec, glu_w.astype(BF16), gavg, w_out.astype(BF16), norm2_w[None, :]]
    specs = [row(d), mod, row(256, 3), row(256), row(256), row(256), row(256), row(256, 4), row(256), row(256),
             row(256, 2), row(256), row(256), _full(vec.shape), _full(glu_w.shape), _full(gavg.shape),
             _full(w_out.shape), _full((1, d))]
    out_specs = [row(d), row(d)]
    out_shape = [jax.ShapeDtypeStruct((bsz, t_all, d), F32), jax.ShapeDtypeStruct((bsz, t_all, d), F32)]
    if router is not None:
        rw = jnp.pad(router[0], ((0, 0), (0, 128 - N_EXPERTS)))
        rw_hi = rw.astype(BF16)
        rw = jnp.stack([rw_hi, (rw - rw_hi.astype(F32)).astype(BF16)])
        rb = jnp.pad(router[1], (0, 128 - N_EXPERTS))[None, :]
        ins += [rw, rb]
        specs += [_full(rw.shape), _full(rb.shape)]
        out_specs.append(row(128))
        out_shape.append(jax.ShapeDtypeStruct((bsz, t_all, 128), F32))
    batched = (True,) * 13 + (False,) * (len(ins) - 13) + (True,) * len(out_specs)
    return pl.pallas_call(
        _per_batch(functools.partial(_out_kernel, with_router=router is not None), batched),
        grid=(bsz // nb, t_all // ROW_TILE),
        in_specs=specs, out_specs=out_specs, out_shape=out_shape,
        compiler_params=_cparams(("arbitrary", "arbitrary")),
    )(*ins)


FF_SPLIT = 2


def _swiglu(tok16, w1_ref, w3_ref, w2_ref):
    d_ff = w1_ref.shape[-1]
    step = d_ff // FF_SPLIT
    acc = None
    for j in range(FF_SPLIT):
        sl = slice(j * step, (j + 1) * step)
        h = _silu(_dot(tok16, w1_ref[:, sl])) * _dot(tok16, w3_ref[:, sl])
        part = _dot(h.astype(BF16), w2_ref[sl, :])
        acc = part if acc is None else acc + part
    return acc


def _ffn_kernel(x_ref, tok_ref, mod_ref, w1_ref, w3_ref, w2_ref, o_ref):
    f = _swiglu(tok_ref[...].astype(BF16), w1_ref, w3_ref, w2_ref)
    o_ref[...] = x_ref[...] + mod_ref[M_GATE2:M_GATE2 + 1, :] * f


def _dense_ffn(xs, tok, mods, w1, w3, w2, n_ctx_tiles):
    bsz, t_all, d = xs.shape
    row, mod = _tile_specs(n_ctx_tiles)
    once = lambda a: pl.BlockSpec(a.shape, lambda b, t: (0, 0), pipeline_mode=pl.Buffered(1))
    w1, w3, w2 = w1.astype(BF16), w3.astype(BF16), w2.astype(BF16)
    return pl.pallas_call(
        _ffn_kernel, grid=(bsz, t_all // ROW_TILE),
        in_specs=[row(d), row(d), mod, once(w1), once(w3), once(w2)],
        out_specs=row(d), out_shape=jax.ShapeDtypeStruct(xs.shape, F32),
        compiler_params=_cparams(("arbitrary", "arbitrary")),
    )(xs, tok, mods, w1, w3, w2)


MOE_BLOCK = 512


def _routing(logits):
    n = logits.shape[0]
    top_v, top_i = lax.top_k(logits[:, :N_EXPERTS], TOP_K)
    gates = jax.nn.softmax(top_v, axis=-1)
    flat_e = top_i.reshape(-1).astype(jnp.int32)
    onehot = (flat_e[:, None] == jnp.arange(N_EXPERTS, dtype=jnp.int32)[None, :]).astype(jnp.int32)
    csum = jnp.cumsum(onehot, axis=0)
    counts = csum[-1]
    rank = jnp.take_along_axis(csum, flat_e[:, None], axis=1)[:, 0] - 1
    padded = (counts + MOE_BLOCK - 1) // MOE_BLOCK * MOE_BLOCK
    pends = jnp.cumsum(padded)
    dest = (pends - padded)[flat_e] + rank
    n_blocks = -(-(n * TOP_K) // MOE_BLOCK) + N_EXPERTS
    cap = n_blocks * MOE_BLOCK
    flat_tok = jnp.arange(n * TOP_K, dtype=jnp.int32) // TOP_K
    row_tok = jnp.zeros((cap,), jnp.int32).at[dest].set(flat_tok)
    block_start = jnp.arange(n_blocks, dtype=jnp.int32) * MOE_BLOCK
    block_e = jnp.minimum(jnp.searchsorted(pends, block_start, side='right'), N_EXPERTS - 1).astype(jnp.int32)
    return dest.reshape(n, TOP_K), gates, row_tok, block_e


def _row_copy(src_hbm, row, dst_ref, sem):
    return pltpu.make_async_copy(src_hbm.at[pl.ds(row, 1), :], dst_ref, sem)


def _rows_wait(src_hbm, dst_ref, sem):
    pltpu.make_async_copy(src_hbm.at[pl.ds(0, dst_ref.shape[0]), :], dst_ref, sem).wait()


def _moe_kernel(be_ref, idx_ref, idx_next_ref, tok_hbm, w1_ref, w3_ref, w2_ref, o_ref, xg_ref, sem):
    del be_ref
    i, j = pl.program_id(0), pl.program_id(1)
    slot = i % 2
    bm = xg_ref.shape[1]
    part_rows = bm // FF_SPLIT

    @pl.when((i == 0) & (j == 0))
    def _():
        def body(r, carry):
            _row_copy(tok_hbm, idx_ref[0, r], xg_ref.at[0, pl.ds(r, 1), :], sem.at[0]).start()
            return carry
        lax.fori_loop(0, bm, body, 0)

    @pl.when(j == 0)
    def _():
        _rows_wait(tok_hbm, xg_ref.at[slot], sem.at[slot])

    for jj in range(FF_SPLIT):
        @pl.when(j == jj)
        def _(jj=jj):
            for r in range(jj * part_rows, (jj + 1) * part_rows):
                _row_copy(tok_hbm, idx_next_ref[0, r], xg_ref.at[1 - slot, pl.ds(r, 1), :], sem.at[1 - slot]).start()

    x16 = xg_ref[slot].astype(BF16)
    h = _silu(_dot(x16, w1_ref[...])) * _dot(x16, w3_ref[...])
    part = _dot(h.astype(BF16), w2_ref[...])

    @pl.when(j == 0)
    def _():
        o_ref[...] = part

    @pl.when(j > 0)
    def _():
        o_ref[...] += part

    @pl.when((i == pl.num_programs(0) - 1) & (j == FF_SPLIT - 1))
    def _():
        _rows_wait(tok_hbm, xg_ref.at[1 - slot], sem.at[1 - slot])


def _moe_experts(tok, row_tok, block_e, w1, w3, w2, layer):
    n, d = tok.shape
    bm = MOE_BLOCK
    n_blocks = block_e.shape[0]
    d_ff = w1.shape[-1]
    step = d_ff // FF_SPLIT
    idx = row_tok.reshape(n_blocks, 1, bm)
    smem_blk = lambda f: pl.BlockSpec((None, 1, bm), f, memory_space=pltpu.SMEM)
    grid_spec = pltpu.PrefetchScalarGridSpec(
        num_scalar_prefetch=1, grid=(n_blocks, FF_SPLIT),
        in_specs=[
            smem_blk(lambda i, j, be: (i, 0, 0)),
            smem_blk(lambda i, j, be: (jnp.minimum(i + 1, n_blocks - 1), 0, 0)),
            pl.BlockSpec(memory_space=pl.ANY),
            pl.BlockSpec((None, None, d, step), lambda i, j, be: (layer, be[i], 0, j)),
            pl.BlockSpec((None, None, d, step), lambda i, j, be: (layer, be[i], 0, j)),
            pl.BlockSpec((None, None, step, d), lambda i, j, be: (layer, be[i], j, 0)),
        ],
        out_specs=pl.BlockSpec((bm, d), lambda i, j, be: (i, 0)),
        scratch_shapes=[pltpu.VMEM((2, bm, d), F32), pltpu.SemaphoreType.DMA((2,))],
    )
    return pl.pallas_call(
        _moe_kernel, grid_spec=grid_spec,
        out_shape=jax.ShapeDtypeStruct((n_blocks * bm, d), F32),
        compiler_params=_cparams(("arbitrary", "arbitrary")),
    )(block_e, idx, idx, tok, w1, w3, w2)


def _combine_kernel(d_ref, dn_ref, x_ref, g_ref, mod_ref, y_hbm, *rest, final):
    fw_ref, (o_ref, buf_ref, sem) = (rest[0], rest[1:]) if final else (None, rest)
    i = pl.program_id(0)
    slot = i % 2
    tm = x_ref.shape[0]

    def gather(idx, to_slot):
        def body(r, carry):
            for k in range(TOP_K):
                _row_copy(y_hbm, idx[k, r], buf_ref.at[to_slot, pl.ds(k * tm + r, 1), :], sem.at[to_slot]).start()
            return carry
        lax.fori_loop(0, tm, body, 0, unroll=8)

    @pl.when(i == 0)
    def _():
        gather(d_ref, 0)

    @pl.when(i + 1 < pl.num_programs(0))
    def _():
        gather(dn_ref, 1 - slot)

    _rows_wait(y_hbm, buf_ref.at[slot], sem.at[slot])
    f = buf_ref[slot, 0:tm, :] * g_ref[:, 0:1] + buf_ref[slot, tm:2 * tm, :] * g_ref[:, 1:2]
    out = x_ref[...] + mod_ref[M_GATE2:M_GATE2 + 1, :] * f
    o_ref[...] = _rms(out, fw_ref[...]) if final else out


def _moe_combine(xs, mods, ybuf, dest, gates, n_ctx_tiles, final_w=None):
    bsz, t_all, d = xs.shape
    tm = ROW_TILE
    tiles_b = t_all // tm
    n_tiles = bsz * tiles_b
    final = final_w is not None
    skip = n_ctx_tiles if final else 0
    per_b = tiles_b - skip
    n_steps = bsz * per_b
    src = lambda i: (i // per_b) * tiles_b + skip + i % per_b
    dt = dest.reshape(n_tiles, tm, TOP_K).transpose(0, 2, 1)
    smem_blk = lambda f: pl.BlockSpec((None, TOP_K, tm), f, memory_space=pltpu.SMEM)
    ins = [dt, dt, xs.reshape(bsz * t_all, d), gates, mods, ybuf]
    in_specs = [
        smem_blk(lambda i: (src(i), 0, 0)),
        smem_blk(lambda i: (src(jnp.minimum(i + 1, n_steps - 1)), 0, 0)),
        pl.BlockSpec((tm, d), lambda i: (src(i), 0)),
        pl.BlockSpec((tm, TOP_K), lambda i: (src(i), 0)),
        pl.BlockSpec((None, None, 8, d),
                     lambda i: (i // per_b, jnp.where(src(i) % tiles_b >= n_ctx_tiles, 1, 0), 0, 0)),
        pl.BlockSpec(memory_space=pl.ANY),
    ]
    if final:
        ins.append(final_w[None, :])
        in_specs.append(_full((1, d)))
    out = pl.pallas_call(
        functools.partial(_combine_kernel, final=final), grid=(n_steps,),
        in_specs=in_specs,
        out_specs=pl.BlockSpec((tm, d), lambda i: (i, 0)),
        out_shape=jax.ShapeDtypeStruct((n_steps * tm, d), F32),
        scratch_shapes=[pltpu.VMEM((2, TOP_K * tm, d), F32), pltpu.SemaphoreType.DMA((2,))],
        compiler_params=_cparams(("arbitrary",)),
    )(*ins)
    return out.reshape(bsz, per_b * tm, d)


def _final_kernel(x_ref, w_ref, o_ref):
    o_ref[...] = _rms(x_ref[...], w_ref[...])


def _final_norm(xs, w, n_ctx_tiles):
    bsz, t_all, d = xs.shape
    tm = ROW_TILE
    n_lat = t_all // tm - n_ctx_tiles
    return pl.pallas_call(
        _final_kernel, grid=(bsz, n_lat),
        in_specs=[pl.BlockSpec((None, tm, d), lambda b, t: (b, t + n_ctx_tiles, 0)), _full((1, d))],
        out_specs=pl.BlockSpec((None, tm, d), lambda b, t: (b, t, 0)),
        out_shape=jax.ShapeDtypeStruct((bsz, n_lat * tm, d), F32),
        compiler_params=_cparams(("arbitrary", "arbitrary")),
    )(xs, w[None, :])


def kernel(x, c, ctx, c_ctx, ada_w, ada_b, norm1_w, norm2_w, w_in, w_out, ret_gn_w, s5_lam_re, s5_lam_im, s5_log_dt, s5_b_re, s5_b_im, s5_c_re, s5_c_im, s5_d, s5_glu_w, s5_glu_b, hg_lb_logits, hg_norm_w, gla_wa2, gla_ba, gla_norm_w, ffn_w1, ffn_w3, ffn_w2, router_w, router_b, moe_w1, moe_w3, moe_w2, final_norm_w):
    bsz, seq, d = x.shape
    ctx_len = ctx.shape[1]
    depth = ada_w.shape[0]
    assert d == D_MODEL and ctx_len % ROW_TILE == 0 and seq % ROW_TILE == 0
    assert ROW_TILE == SCAN_CHUNK and seq % GRID_W == 0
    n_ctx_tiles = ctx_len // ROW_TILE

    xs = jnp.concatenate([ctx, x], axis=1)
    mods = _modulation(c, c_ctx, ada_w, ada_b)
    cos, sin = _rope_tables(seq, ctx_len)
    lb_sm = jax.nn.softmax(hg_lb_logits.astype(F32), axis=0)
    lower_bounds = jnp.clip(jnp.cumsum(lb_sm, axis=0) - lb_sm[0], 0.0, LB_CEIL)
    w_in_pad = jnp.pad(w_in, ((0, 0), (0, 0), (0, D_IN_PAD - D_IN))).astype(BF16)
    s5_ops = _s5_operators(s5_lam_re, s5_lam_im, s5_log_dt, s5_b_re, s5_b_im, s5_c_re, s5_c_im)
    zeros = jnp.zeros((depth, 3, MIX_W), F32)
    lb_rows = jnp.concatenate([jnp.maximum(lower_bounds, LB_FLOOR)[:, None], (1.0 - lower_bounds)[:, None],
                               zeros, zeros], axis=1)
    vecs = jnp.stack([ret_gn_w, hg_norm_w, gla_norm_w, s5_d, s5_glu_b], axis=1)
    vecs = jnp.concatenate([vecs, zeros], axis=1)
    wa_pad = jnp.stack([jnp.pad(gla_wa2[:, dr], ((0, 0), (GLA_RANK * dr, 128 - GLA_RANK * (dr + 1)), (0, 0)))
                        for dr in range(2)], axis=1)
    w_out16, glu16 = w_out.astype(BF16), s5_glu_w.astype(BF16)
    moe16 = [w.astype(BF16) for w in (moe_w1, moe_w3, moe_w2)]

    for l in range(depth):
        p_ret, p_s5, p_hg, p_gla = _project(xs, mods[l], norm1_w[l], w_in_pad[l], n_ctx_tiles)

        yrf = _ret_scan(p_ret, cos, sin, n_ctx_tiles, False)
        yrb = _ret_scan(p_ret, cos, sin, n_ctx_tiles, True)
        ys5 = _s5_mix(p_s5, *(o[l] for o in s5_ops), ctx_len)
        yhf = _hg_scan(p_hg, lb_rows[l], n_ctx_tiles, False)
        yhb = _hg_scan(p_hg, lb_rows[l], n_ctx_tiles, True)
        ygs = [_gla_scan(p_gla, wa_pad[l, dr], gla_ba[l, dr][None, :], n_ctx_tiles, bool(dr)) for dr in range(2)]

        j = l // 2
        router = (router_w[j], router_b[j]) if l % 2 == 1 else None
        outs = _mix_out(xs, mods[l], p_ret, yrf, yrb, p_s5, ys5, p_hg, yhf, yhb, p_gla, ygs[0], ygs[1],
                        vecs[l], glu16[l], w_out16[l], norm2_w[l], router, n_ctx_tiles)
        if l % 2 == 0:
            x_new, tok = outs
            xs = _dense_ffn(x_new, tok, mods[l], ffn_w1[j], ffn_w3[j], ffn_w2[j], n_ctx_tiles)
        else:
            x_new, tok, logits = outs
            n_tok = bsz * xs.shape[1]
            dest, gates, row_tok, block_e = _routing(logits.reshape(n_tok, 128))
            ybuf = _moe_experts(tok.reshape(n_tok, d), row_tok, block_e, *moe16, j)
            if l == depth - 1:
                return _moe_combine(x_new, mods[l], ybuf, dest, gates, n_ctx_tiles, final_w=final_norm_w)
            xs = _moe_combine(x_new, mods[l], ybuf, dest, gates, n_ctx_tiles)

    return _final_norm(xs, final_norm_w, n_ctx_tiles)
```

```python
import functools
import math

import numpy as np
import jax
import jax.numpy as jnp
from jax import lax
from jax.experimental import pallas as pl
from jax.experimental.pallas import tpu as pltpu

F32 = jnp.float32
BF16 = jnp.bfloat16

D_MODEL = 1024
GRID_W = 64
MIX_W = 256
NORM_EPS = 1e-6
ROPE_BASE = 10000.0
LB_FLOOR = 1e-30
LB_CEIL = 1.0 - 1e-6
N_HEADS = 4
S5_GROUP = 16
S5_GROUPS = 16
S5_STATE = 64
S5_SUB = 16
GLA_RANK = 16
GLA_TAU = 16.0
N_EXPERTS = 8
TOP_K = 2
D_IN = 3360
D_IN_PAD = 3456

SCAN_CHUNK = 256
SCAN_SUB = 16
TAME_LIMIT = 40.0
NEG_BIG = -1e30
VMEM_LIMIT = 56 * 1024 * 1024


def _cparams(sem):
    return pltpu.CompilerParams(dimension_semantics=sem, vmem_limit_bytes=VMEM_LIMIT)


def _full(shape):
    n = len(shape)
    return pl.BlockSpec(shape, lambda *_: (0,) * n)


def _dot(a, b):
    return jnp.dot(a, b, preferred_element_type=F32)


def _dot_nt(a, b):
    return lax.dot_general(a, b, (((1,), (1,)), ((), ())), preferred_element_type=F32)


def _dot_hi(a, b):
    return jnp.dot(a, b, preferred_element_type=F32, precision=lax.Precision.HIGHEST)


def _sigmoid(z):
    return 0.5 * (jnp.tanh(0.5 * z) + 1.0)


def _silu(z):
    return z * _sigmoid(z)


def _log_sigmoid(z):
    return jnp.minimum(z, 0.0) - jnp.log(1.0 + jnp.exp(-jnp.abs(z)))


def _scan_constants(n_heads, wk, wv, reverse):
    c, s, hc = SCAN_CHUNK, SCAN_SUB, SCAN_CHUNK // 2
    i = np.arange(c)[:, None]
    r = np.arange(c)[None, :]
    cum = (r >= i) if reverse else (r <= i)
    ih = np.arange(hc)[:, None]
    jh = np.arange(hc)[None, :]
    causal = (jh >= ih) if reverse else (jh <= ih)
    masks = [((ih // s) == (jh // s)) & causal]
    m = s
    while m < hc:
        same = (ih // (2 * m)) == (jh // (2 * m))
        i_hi, j_hi = (ih % (2 * m)) >= m, (jh % (2 * m)) >= m
        masks.append(same & (~i_hi & j_hi if reverse else i_hi & ~j_hi))
        m *= 2
    dk, dv = wk // n_heads, wv // n_heads
    lk, lv = np.arange(wk), np.arange(wv)
    hmv = (lv[None, :] // dv) == np.arange(n_heads)[:, None]
    bd = (lv[:, None] // dv) == (lk[None, :] // dk)
    return dict(
        cm=jnp.asarray(np.concatenate([cum] * 2, axis=1), BF16),
        pm=jnp.asarray(np.stack([np.tile(x, (1, n_heads)) for x in masks]), F32),
        hmv=jnp.asarray(hmv[:, None, :], BF16),
        bd=jnp.asarray(bd, F32),
        gmat=jnp.asarray(bd.T, BF16),
    )


def _head_scores(qd, k16, n_heads):
    r, wk = qd.shape
    dk = wk // n_heads
    per_col = 128 // dk
    lane_head = lax.broadcasted_iota(jnp.int32, (1, 128), 1) // dk
    blocks = []
    for col in range(wk // 128):
        qc = qd[:, col * 128:(col + 1) * 128]
        lhs = jnp.concatenate([jnp.where(lane_head == h, qc, 0.0) for h in range(per_col)], axis=0)
        sc = _dot_nt(lhs.astype(BF16), k16[:, col * 128:(col + 1) * 128])
        blocks += [sc[h * r:(h + 1) * r, :] for h in range(per_col)]
    return jnp.concatenate(blocks, axis=1)


def _ref_rows(bb_ref, rows, span):
    wk = bb_ref.shape[1]
    pieces = [jnp.zeros((span, wk), F32) if r is None else jnp.broadcast_to(bb_ref[r:r + 1, :], (span, wk))
              for r in rows]
    return pieces[0] if len(pieces) == 1 else jnp.concatenate(pieces, axis=0)


def _scan_core(q, k, v, g, cm_ref, pm_ref, hmv_ref, bd_ref, st_ref, bb_ref, o_ref, *, reverse):
    c, s, hc = SCAN_CHUNK, SCAN_SUB, SCAN_CHUNK // 2
    wk, wv = q.shape[1], v.shape[1]
    n_heads = hmv_ref.shape[0]
    g_hi = g.astype(BF16)
    g_lo = (g - g_hi.astype(F32)).astype(BF16)
    b = _dot(cm_ref[...], jnp.concatenate([g_hi, g_lo], axis=0))
    bb_ref[...] = b
    end = 0 if reverse else c - 1
    b_end = bb_ref[end:end + 1, :]
    st = st_ref[...]
    o_inter = _dot_nt((q * jnp.exp(b)).astype(BF16), st.astype(BF16))
    upd = _dot(v.T.astype(BF16), (k * jnp.exp(b_end - b)).astype(BF16))
    st_ref[...] = st * jnp.exp(b_end) + upd * bd_ref[...]

    v16 = v.astype(BF16)
    vexp = [_expand_heads(v16[h0:h0 + hc], hmv_ref[...]) for h0 in (0, hc)]
    halves = (slice(0, hc), slice(hc, c))

    n_blk = c // s
    if reverse:
        rows = [(i + 1) * s for i in range(n_blk - 1)] + [None]
    else:
        rows = [None] + [i * s - 1 for i in range(1, n_blk)]
    dq = b - _ref_rows(bb_ref, rows, s)
    low = jnp.min(jnp.min(dq, axis=1, keepdims=True), axis=0, keepdims=True)
    qd = q * jnp.exp(dq)
    k16 = (k * jnp.exp(jnp.minimum(-dq, TAME_LIMIT))).astype(BF16)
    pm_diag = pm_ref[0] * (low >= -TAME_LIMIT).astype(F32)
    a = [_head_scores(qd[h], k16[h], n_heads) * pm_diag for h in halves]

    m, lev = s, 1
    while m < hc:
        n_pair = c // (2 * m)
        rows = [p * 2 * m + (m if reverse else m - 1) for p in range(n_pair)]
        ed = jnp.exp(-jnp.abs(b - _ref_rows(bb_ref, rows, 2 * m)))
        qd, k16 = q * ed, (k * ed).astype(BF16)
        a = [a[i] + _head_scores(qd[h], k16[h], n_heads) * pm_ref[lev] for i, h in enumerate(halves)]
        m, lev = 2 * m, lev + 1

    ed = jnp.exp(-jnp.abs(b - _ref_rows(bb_ref, [hc if reverse else hc - 1], c)))
    qd, k16 = q * ed, (k * ed).astype(BF16)
    late, early = (0, 1) if reverse else (1, 0)
    top = _head_scores(qd[halves[late]], k16[halves[early]], n_heads)

    o_ref[halves[early], :] = o_inter[halves[early]] + _dot(a[early].astype(BF16), vexp[early])
    a_late = jnp.concatenate([a[late], top], axis=1).astype(BF16)
    o_ref[halves[late], :] = o_inter[halves[late]] + _dot(a_late, jnp.concatenate([vexp[late], vexp[early]], axis=0))

    return low[0, 0] < -TAME_LIMIT


def _explicit_diagonal(q, k, v, gmat_ref, bb_ref, qb_ref, kb_ref, vb_ref, wj_ref, o_ref, reverse):
    c, s = SCAN_CHUNK, SCAN_SUB
    wv = v.shape[1]
    qb_ref[...] = q
    kb_ref[...] = k
    vb_ref[...] = v
    rows = lax.broadcasted_iota(jnp.int32, (s, 1), 0)

    def blk(ib, carry):
        base = pl.multiple_of(ib * s, s)
        bblk = bb_ref[pl.ds(base, s), :]
        qblk = qb_ref[pl.ds(base, s), :]
        for jo in range(s):
            bj = bb_ref[pl.ds(base + jo, 1), :]
            kj = kb_ref[pl.ds(base + jo, 1), :]
            valid = (rows <= jo) if reverse else (rows >= jo)
            w = qblk * kj * jnp.exp(jnp.where(valid, bblk - bj, NEG_BIG))
            wj_ref[pl.ds(jo * s, s), :] = w.astype(BF16)
        r = _dot(wj_ref[...], gmat_ref[...])
        acc = jnp.zeros((s, wv), F32)
        for jo in range(s):
            acc = acc + r[jo * s:(jo + 1) * s, :] * vb_ref[pl.ds(base + jo, 1), :]
        o_ref[pl.ds(base, s), :] += acc
        return carry

    lax.fori_loop(0, c // s, blk, 0)


def _rope(t, cos, sin_signed):
    outs = []
    for h in range(t.shape[1] // 128):
        th = t[:, h * 128:(h + 1) * 128]
        lane = lax.broadcasted_iota(jnp.int32, th.shape, 1)
        partner = jnp.where((lane % 32) < 16, pltpu.roll(th, 128 - 16, 1), pltpu.roll(th, 16, 1))
        outs.append(th * cos[:, h * 128:(h + 1) * 128] + partner * sin_signed[:, h * 128:(h + 1) * 128])
    return jnp.concatenate(outs, axis=1)


def _expand_heads(x16, hm):
    return (x16[None] * hm).reshape(hm.shape[0] * x16.shape[0], x16.shape[1])


def _ret_constants(reverse):
    c, hc = SCAN_CHUNK, SCAN_CHUNK // 2
    lg = jnp.log1p(-(2.0 ** (-5.0 - jnp.arange(N_HEADS, dtype=F32))))
    lg = lg[::-1] if reverse else lg
    lane = jnp.repeat(lg, MIX_W // N_HEADS)[None, :]
    i = jnp.arange(c, dtype=F32)[:, None]
    steps_in = (c - i) if reverse else (i + 1.0)
    steps_out = i if reverse else (c - 1.0 - i)
    dec = jnp.concatenate([jnp.exp(lane * steps_in), jnp.exp(lane * steps_out),
                           jnp.broadcast_to(jnp.exp(lane * c), (8, MIX_W))], axis=0)
    ih = jnp.arange(hc, dtype=F32)[:, None]
    jh = jnp.arange(hc, dtype=F32)[None, :]
    dist = (jh - ih) if reverse else (ih - jh)
    per_head = lambda d: jnp.concatenate(
        [jnp.where(d >= 0, jnp.exp(lg[h] * jnp.maximum(d, 0.0)), 0.0) for h in range(N_HEADS)], axis=1)
    dm = jnp.stack([per_head(dist), per_head(dist + hc)])
    hd = np.arange(MIX_W) // (MIX_W // N_HEADS)
    hm = jnp.asarray((hd[None, :] == np.arange(N_HEADS)[:, None])[:, None, :], BF16)
    bd = jnp.asarray(hd[:, None] == hd[None, :], F32)
    return dec, dm, hm, bd


SCAN_BATCH = 4


def _ret_kernel(p_ref, cos_ref, sin_ref, dec_ref, dm_ref, hm_ref, bd_ref, o_ref, st_ref, *, reverse):
    @pl.when(pl.program_id(1) == 0)
    def _():
        st_ref[...] = jnp.zeros_like(st_ref)

    c, hc = SCAN_CHUNK, SCAN_CHUNK // 2
    cos, sin = cos_ref[...], sin_ref[...]
    hm = hm_ref[...]
    halves = (slice(0, hc), slice(hc, c))
    late, early = (0, 1) if reverse else (1, 0)
    for bi in range(p_ref.shape[0]):
        q = _rope(p_ref[bi, :, 0:256], cos, sin)
        k = _rope(p_ref[bi, :, 256:512], cos, sin) * (64.0 ** -0.5)
        v = p_ref[bi, :, 512:768]
        st = st_ref[bi]
        o_inter = _dot_nt((q * dec_ref[0:c, :]).astype(BF16), st.astype(BF16))
        upd = _dot(v.T.astype(BF16), (k * dec_ref[c:2 * c, :]).astype(BF16))
        st_ref[bi] = st * dec_ref[2 * c:2 * c + 1, :] + upd * bd_ref[...]

        k16, v16 = k.astype(BF16), v.astype(BF16)
        vx = [_expand_heads(v16[h], hm) for h in halves]
        a_early = _head_scores(q[halves[early]], k16[halves[early]], N_HEADS) * dm_ref[0]
        a_late = _head_scores(q[halves[late]], k16[halves[late]], N_HEADS) * dm_ref[0]
        top = _head_scores(q[halves[late]], k16[halves[early]], N_HEADS) * dm_ref[1]
        o_ref[bi, halves[early], :] = o_inter[halves[early]] + _dot(a_early.astype(BF16), vx[early])
        a_late = jnp.concatenate([a_late, top], axis=1).astype(BF16)
        o_ref[bi, halves[late], :] = (o_inter[halves[late]]
                                      + _dot(a_late, jnp.concatenate([vx[late], vx[early]], axis=0)))


def _hg_inputs(p_ref, extra, reverse):
    (lb_ref,) = extra
    q = p_ref[:, 0:256]
    z = p_ref[:, 512:768] if reverse else p_ref[:, 256:512]
    v = p_ref[:, 768:1024]
    lb_floor, one_m_lb = lb_ref[0:1, :], lb_ref[1:2, :]
    sg = _sigmoid(z)
    return q, one_m_lb * (1.0 - sg), v, jnp.log(lb_floor + one_m_lb * sg)


def _gla_inputs(p_ref, extra, reverse):
    wa_ref, ba_ref = extra
    q = p_ref[:, 0:128]
    k = p_ref[:, 128:256] * (32.0 ** -0.5)
    v = p_ref[:, 256:512]
    g = _log_sigmoid(_dot_hi(p_ref[:, 768:896], wa_ref[...]) + ba_ref[...]) * (1.0 / GLA_TAU)
    return q, k, v, g


def _scan_kernel(p_ref, *rest, inputs_fn, n_extra, reverse):
    extra, rest = rest[:n_extra], rest[n_extra:]
    (cm_ref, pm_ref, hmv_ref, bd_ref, gmat_ref, o_ref, st_ref, bb_ref, *fallback_scratch) = rest

    @pl.when(pl.program_id(1) == 0)
    def _():
        st_ref[...] = jnp.zeros_like(st_ref)

    n_b = p_ref.shape[0]
    untame = []
    for bi in range(n_b):
        q, k, v, g = inputs_fn(p_ref.at[bi], extra, reverse)
        untame.append(_scan_core(q, k, v, g, cm_ref, pm_ref, hmv_ref, bd_ref,
                                 st_ref.at[bi], bb_ref.at[bi], o_ref.at[bi], reverse=reverse))
    for bi in range(n_b):
        @pl.when(untame[bi])
        def _(bi=bi):
            q, k, v, _ = inputs_fn(p_ref.at[bi], extra, reverse)
            _explicit_diagonal(q, k, v, gmat_ref, bb_ref.at[bi], *fallback_scratch, o_ref.at[bi], reverse)


def _scan_tile_map(n_tiles, n_ctx_tiles, reverse):
    if not reverse:
        return lambda step: step
    return lambda step: jnp.where(step < n_ctx_tiles, n_ctx_tiles - 1 - step, n_tiles - 1 - (step - n_ctx_tiles))


def _scan_call(inputs_fn, p, extra, wk, wv, n_ctx_tiles, reverse):
    bsz, t_all, wp = p.shape
    c, s, nb = SCAN_CHUNK, SCAN_SUB, math.gcd(SCAN_BATCH, bsz)
    n_tiles = t_all // c
    cst = _scan_constants(N_HEADS, wk, wv, reverse)
    tile = _scan_tile_map(n_tiles, n_ctx_tiles, reverse)
    consts = [cst[n] for n in ('cm', 'pm', 'hmv', 'bd', 'gmat')]
    in_specs = [pl.BlockSpec((nb, c, wp), lambda b, t: (b, tile(t), 0))]
    in_specs += [_full(x.shape) for x in extra] + [_full(x.shape) for x in consts]
    return pl.pallas_call(
        functools.partial(_scan_kernel, inputs_fn=inputs_fn, n_extra=len(extra), reverse=reverse),
        grid=(bsz // nb, n_tiles),
        in_specs=in_specs,
        out_specs=pl.BlockSpec((nb, c, wv), lambda b, t: (b, tile(t), 0)),
        out_shape=jax.ShapeDtypeStruct((bsz, t_all, wv), F32),
        scratch_shapes=[
            pltpu.VMEM((nb, wv, wk), F32), pltpu.VMEM((nb, c, wk), F32),
            pltpu.VMEM((c, wk), F32), pltpu.VMEM((c, wk), F32), pltpu.VMEM((c, wv), F32),
            pltpu.VMEM((s * s, wk), BF16),
        ],
        compiler_params=_cparams(("arbitrary", "arbitrary")),
    )(p, *extra, *consts)


def _ret_scan(p_ret, cos, sin, n_ctx_tiles, reverse):
    bsz, t_all, wp = p_ret.shape
    c, nb = SCAN_CHUNK, math.gcd(SCAN_BATCH, bsz)
    tile = _scan_tile_map(t_all // c, n_ctx_tiles, reverse)
    consts = _ret_constants(reverse)
    rows = lambda w: pl.BlockSpec((c, w), lambda b, t: (tile(t), 0))
    return pl.pallas_call(
        functools.partial(_ret_kernel, reverse=reverse),
        grid=(bsz // nb, t_all // c),
        in_specs=[pl.BlockSpec((nb, c, wp), lambda b, t: (b, tile(t), 0)), rows(MIX_W), rows(MIX_W)]
        + [_full(x.shape) for x in consts],
        out_specs=pl.BlockSpec((nb, c, MIX_W), lambda b, t: (b, tile(t), 0)),
        out_shape=jax.ShapeDtypeStruct((bsz, t_all, MIX_W), F32),
        scratch_shapes=[pltpu.VMEM((nb, MIX_W, MIX_W), F32)],
        compiler_params=_cparams(("arbitrary", "arbitrary")),
    )(p_ret, cos, sin, *consts)


def _hg_scan(p_hg, lb_rows, n_ctx_tiles, reverse):
    return _scan_call(_hg_inputs, p_hg, [lb_rows], 256, 256, n_ctx_tiles, reverse)


def _gla_scan(p_gla, wa, ba, n_ctx_tiles, reverse):
    return _scan_call(_gla_inputs, p_gla, [wa, ba], 128, 256, n_ctx_tiles, reverse)


def _rope_tables(seq, ctx_len):
    t_idx = jnp.arange(seq, dtype=jnp.int32)
    row = (t_idx // GRID_W).astype(F32)
    col = (t_idx % GRID_W).astype(F32)
    freqs = ROPE_BASE ** (-jnp.arange(16, dtype=F32) / 16)
    ang_r = row[:, None] * freqs[None, :]
    ang_c = col[:, None] * freqs[None, :]
    cos = jnp.concatenate([jnp.cos(ang_r)] * 2 + [jnp.cos(ang_c)] * 2, axis=1)
    sin = jnp.concatenate([-jnp.sin(ang_r), jnp.sin(ang_r), -jnp.sin(ang_c), jnp.sin(ang_c)], axis=1)
    cos = jnp.concatenate([jnp.ones((ctx_len, 64), F32), cos], axis=0)
    sin = jnp.concatenate([jnp.zeros((ctx_len, 64), F32), sin], axis=0)
    return jnp.tile(cos, (1, N_HEADS)), jnp.tile(sin, (1, N_HEADS))


def _s5_response_kernel(k_ref, rep_ref, o_ref):
    n = S5_SUB
    size = n * S5_GROUP
    diff = (lax.broadcasted_iota(jnp.int32, (size, size), 1) // S5_GROUP
            - lax.broadcasted_iota(jnp.int32, (size, size), 0) // S5_GROUP)

    def tiled(block):
        return jnp.concatenate([_dot_hi(block, rep_ref[...])] * n, axis=0)

    m = jnp.where(diff == 0, tiled(k_ref[0, 0] + k_ref[1, 0]), 0.0)
    for tau in range(1, n):
        m = jnp.where(diff == tau, tiled(k_ref[0, tau]), m)
        m = jnp.where(diff == -tau, tiled(k_ref[1, tau]), m)
    o_ref[...] = m.astype(o_ref.dtype)


def _s5_response(kern):
    depth, _, g, n, cg, _ = kern.shape
    rep = jnp.asarray(np.tile(np.eye(cg), (1, n)), F32)
    return pl.pallas_call(
        _s5_response_kernel, grid=(depth, g),
        in_specs=[pl.BlockSpec((None, 2, None, n, cg, cg), lambda l, gi: (l, 0, gi, 0, 0, 0)), _full(rep.shape)],
        out_specs=pl.BlockSpec((None, None, n * cg, n * cg), lambda l, gi: (l, gi, 0, 0)),
        out_shape=jax.ShapeDtypeStruct((depth, g, n * cg, n * cg), BF16),
        compiler_params=_cparams(("arbitrary", "arbitrary")),
    )(kern, rep)


def _s5_operators(lam_re, lam_im, log_dt, b_re, b_im, c_re, c_im):
    n = S5_SUB
    lam = lax.complex(lam_re, lam_im)
    log_a = lam * jnp.exp(log_dt)[..., None]
    a_bar = jnp.exp(log_a)
    b_bar = ((a_bar - 1) / lam)[..., None] * lax.complex(b_re, b_im)
    c_mat = lax.complex(c_re, c_im)
    step = np.arange(n)
    pw = jnp.exp(log_a[..., None, :] * jnp.arange(n + 1, dtype=F32)[:, None])
    kern = jnp.einsum('...cp,...tp,...pd->...tdc', c_mat, pw[..., :n, :], b_bar).real
    m = _s5_response(kern)
    qc =pw[..., n - 1 - step, :][..., None] * b_bar[..., None, :, :]
    qc = jnp.swapaxes(qc, -1, -2)
    z = c_mat[..., None, :, :] * pw[..., step + 1, :][..., None, :]
    z = jnp.moveaxis(z, -1, -3)
    qc = jnp.stack([qc[:, 0], jnp.flip(qc[:, 1], axis=-3)], axis=1)
    z = jnp.stack([z[:, 0], jnp.flip(z[:, 1], axis=-2)], axis=1)
    lead = qc.shape[:3]
    qc =qc.reshape(*lead, n * S5_GROUP, S5_STATE)
    z = z.reshape(*lead, S5_STATE, n * S5_GROUP)
    q = jnp.concatenate([qc.real, qc.imag, qc.imag, qc.real], axis=-1)
    pm = jnp.concatenate([z.real, -z.imag, jnp.zeros_like(z.real), jnp.zeros_like(z.real)], axis=-2)
    a_row = pw[..., n, :]
    a1 = jnp.concatenate([a_row.real] * 4, axis=-1)
    a2 = jnp.concatenate([-a_row.imag, a_row.imag, a_row.imag, -a_row.imag], axis=-1)
    a = jnp.stack([a1[:, 0], a2[:, 0], a1[:, 1], a2[:, 1]], axis=1)
    return (m, jnp.concatenate([q[:, 0], q[:, 1]], axis=-1).astype(BF16),
            pm[:, 0].astype(BF16), pm[:, 1].astype(BF16), a)


S5_GROUP_BLOCK = 8


S5_TILE = S5_SUB * S5_SUB


def _s5_in_kernel(u_ref, q_ref, wf_ref, wb_ref, uf_ref):
    gl = S5_GROUP_BLOCK * S5_GROUP

    def regroup(t, carry):
        x3 = u_ref[pl.ds(pl.multiple_of(t * S5_TILE, S5_TILE), S5_TILE), :].astype(BF16).reshape(S5_SUB, S5_SUB, gl)
        for k in range(S5_GROUP_BLOCK):
            piece = x3[:, :, k * S5_GROUP:(k + 1) * S5_GROUP].reshape(S5_SUB, S5_SUB * S5_GROUP)
            uf_ref[k, pl.ds(pl.multiple_of(t * S5_SUB, S5_SUB), S5_SUB), :] = piece
        return carry

    lax.fori_loop(0, u_ref.shape[0] // S5_TILE, regroup, 0)
    for k in range(S5_GROUP_BLOCK):
        w = _dot(uf_ref[k], q_ref[k])
        wf_ref[:, k, :] = w[:, :256]
        wb_ref[:, k, :] = w[:, 256:]


def _s5_state_kernel(wf_ref, wb_ref, a_ref, hf_ref, hb_ref, sf_ref, sb_ref):
    @pl.when(pl.program_id(0) == 0)
    def _():
        sf_ref[...] = jnp.zeros_like(sf_ref)
        sb_ref[...] = jnp.zeros_like(sb_ref)

    def swap(h):
        return jnp.concatenate([h[:, 128:], h[:, :128]], axis=1)

    n = wf_ref.shape[0]
    hf, hb = sf_ref[...], sb_ref[...]
    a1f, a2f, a1b, a2b = a_ref[0], a_ref[1], a_ref[2], a_ref[3]
    for i in range(n):
        hf_ref[i] = hf
        hf = a1f * hf + a2f * swap(hf) + wf_ref[i]
        j = n - 1 - i
        hb_ref[j] = hb
        hb = a1b * hb + a2b * swap(hb) + wb_ref[j]
    sf_ref[...] = hf
    sb_ref[...] = hb


def _s5_out_kernel(u_ref, m_ref, hf_ref, hb_ref, pf_ref, pb_ref, y_ref, ys_ref, hs_ref):
    for k in range(S5_GROUP_BLOCK):
        hs_ref[k % 2, 0] = hf_ref[:, k, :]
        hs_ref[k % 2, 1] = hb_ref[:, k, :]
        y = _dot(u_ref[k], m_ref[k])
        y = y + _dot(hs_ref[k % 2, 0].astype(BF16), pf_ref[k])
        y = y + _dot(hs_ref[k % 2, 1].astype(BF16), pb_ref[k])
        ys_ref[k] = y

    def regroup(t, carry):
        rows = pl.ds(pl.multiple_of(t * S5_SUB, S5_SUB), S5_SUB)
        pieces = [ys_ref[k, rows, :].reshape(S5_SUB, S5_SUB, S5_GROUP) for k in range(S5_GROUP_BLOCK)]
        y_ref[pl.ds(pl.multiple_of(t * S5_TILE, S5_TILE), S5_TILE), :] = (
            jnp.concatenate(pieces, axis=-1).reshape(S5_TILE, S5_GROUP_BLOCK * S5_GROUP))
        return carry

    lax.fori_loop(0, y_ref.shape[0] // S5_TILE, regroup, 0)


def _s5_mix(u, m, q, pf, pb, a_rows, ctx_len):
    bsz, t_all, _ = u.shape
    n, g = S5_SUB, S5_GROUPS
    n_rows = t_all // n
    blk = 16
    n_steps = n_rows // blk
    n_ctx = (ctx_len // n) // blk
    gb = S5_GROUP_BLOCK
    n_gb = g // gb
    nat_spec = pl.BlockSpec((None, t_all, gb * S5_GROUP), lambda b, gi: (b, 0, gi))
    u_spec = pl.BlockSpec((None, gb, n_rows, 256), lambda b, gi: (b, gi, 0, 0))
    st_spec = pl.BlockSpec((n_rows, gb, 256), lambda b, gi: (0, b * n_gb + gi, 0))
    hshape = jax.ShapeDtypeStruct((n_rows, bsz * g, 256), F32)
    wf, wb, uf = pl.pallas_call(
        _s5_in_kernel, grid=(bsz, n_gb),
        in_specs=[nat_spec, pl.BlockSpec((gb, 256, 512), lambda b, gi: (gi, 0, 0))],
        out_specs=[st_spec, st_spec, u_spec],
        out_shape=[hshape, hshape, jax.ShapeDtypeStruct((bsz, g, n_rows, 256), BF16)],
        compiler_params=_cparams(("arbitrary", "arbitrary")),
    )(u, q)

    a = jnp.tile(a_rows, (1, bsz, 1))
    fwd = lambda s: (s, 0, 0)
    bwd = lambda s: (jnp.where(s < n_ctx, n_ctx - 1 - s, n_steps - 1 - (s - n_ctx)), 0, 0)
    st_block = (blk, bsz * g, 256)
    hf, hb = pl.pallas_call(
        _s5_state_kernel, grid=(n_steps,),
        in_specs=[pl.BlockSpec(st_block, fwd), pl.BlockSpec(st_block, bwd), _full(a.shape)],
        out_specs=[pl.BlockSpec(st_block, fwd), pl.BlockSpec(st_block, bwd)],
        out_shape=[hshape, hshape],
        scratch_shapes=[pltpu.VMEM((bsz * g, 256), F32), pltpu.VMEM((bsz * g, 256), F32)],
        compiler_params=_cparams(("arbitrary",)),
    )(wf, wb, a)

    mat_spec = pl.BlockSpec((gb, 256, 256), lambda b, gi: (gi, 0, 0))
    return pl.pallas_call(
        _s5_out_kernel, grid=(bsz, n_gb),
        in_specs=[u_spec, mat_spec, st_spec, st_spec, mat_spec, mat_spec],
        out_specs=nat_spec,
        out_shape=jax.ShapeDtypeStruct((bsz, t_all, g * S5_GROUP), F32),
        scratch_shapes=[pltpu.VMEM((gb, n_rows, 256), F32), pltpu.VMEM((2, 2, n_rows, 256), F32)],
        compiler_params=_cparams(("arbitrary", "arbitrary")),
    )(uf, m, hf, hb, pf, pb)


ROW_TILE = 256
M_SHIFT1, M_SCALE1, M_GATE1, M_SHIFT2, M_SCALE2, M_GATE2 = range(6)


def _ada_kernel(cond_ref, w_ref, b_ref, o_ref):
    o_ref[...] = _dot_hi(_silu(cond_ref[...]), w_ref[...]) + b_ref[...]


def _modulation(c, c_ctx, ada_w, ada_b):
    depth, d, d6 = ada_w.shape
    bsz = c.shape[0]
    cond = jnp.zeros((8, d), F32).at[:bsz].set(c).at[bsz].set(c_ctx)
    out = pl.pallas_call(
        _ada_kernel, grid=(depth, d6 // d),
        in_specs=[_full((8, d)),
                  pl.BlockSpec((None, d, d), lambda l, j: (l, 0, j)),
                  pl.BlockSpec((None, 1, d), lambda l, j: (l, 0, j))],
        out_specs=pl.BlockSpec((None, 8, d), lambda l, j: (l, 0, j)),
        out_shape=jax.ShapeDtypeStruct((depth, 8, d6), F32),
        compiler_params=_cparams(("arbitrary", "arbitrary")),
    )(cond, ada_w, ada_b[:, None, :])
    out = out.reshape(depth, 8, 6, d)
    lat = out[:, :bsz]
    ctx = jnp.broadcast_to(out[:, bsz][:, None], lat.shape)
    mods = jnp.stack([ctx, lat], axis=2)
    return jnp.pad(mods, ((0, 0), (0, 0), (0, 0), (0, 2), (0, 0)))


def _rms(x, w):
    return x * lax.rsqrt(jnp.mean(x * x, axis=-1, keepdims=True) + NORM_EPS) * w


def _proj_kernel(x_ref, mod_ref, nw_ref, w_ref, ret_ref, s5_ref, hg_ref, gla_ref):
    h = _rms(x_ref[...], nw_ref[...]) * (1.0 + mod_ref[M_SCALE1:M_SCALE1 + 1, :]) + mod_ref[M_SHIFT1:M_SHIFT1 + 1, :]
    p = _dot(h.astype(BF16), w_ref[...])
    ret_ref[...] = p[:, 0:1024]
    s5_ref[...] = p[:, 1024:1280]
    hg_ref[...] = p[:, 1280:2560]
    gla_ref[...] = p[:, 2560:3456]


def _tile_specs(n_ctx_tiles, nb=None):
    tm = ROW_TILE
    row = lambda w, j=0: pl.BlockSpec((nb, tm, w), lambda b, t: (b, t, j))
    mod = pl.BlockSpec((nb, None, 8, D_MODEL), lambda b, t: (b, jnp.where(t >= n_ctx_tiles, 1, 0), 0, 0))
    return row, mod


def _per_batch(fn, batched):
    def wrapped(*refs, **kw):
        n_b = next(r.shape[0] for r, flag in zip(refs, batched) if flag)
        for bi in range(n_b):
            fn(*[r.at[bi] if flag else r for r, flag in zip(refs, batched)], **kw)
    return wrapped


def _project(xs, mods, norm_w, w_pad, n_ctx_tiles):
    bsz, t_all, d = xs.shape
    nb = math.gcd(SCAN_BATCH, bsz)
    row, mod = _tile_specs(n_ctx_tiles, nb)
    widths = (1024, 256, 1280, 896)
    return pl.pallas_call(
        _per_batch(_proj_kernel, (True, True, False, False) + (True,) * len(widths)),
        grid=(bsz // nb, t_all // ROW_TILE),
        in_specs=[row(d), mod, _full((1, d)), _full(w_pad.shape)],
        out_specs=[row(w) for w in widths],
        out_shape=[jax.ShapeDtypeStruct((bsz, t_all, w), F32) for w in widths],
        compiler_params=_cparams(("arbitrary", "arbitrary")),
    )(xs, mods, norm_w[None, :], w_pad)


def _dot_split(x, mat):
    hi = x.astype(BF16)
    lo = (x - hi.astype(F32)).astype(BF16)
    return _dot(hi, mat) + _dot(lo, mat)


def _head_norm(y, gavg, w, center):
    if center:
        y = y - _dot_split(y, gavg)
    return y * lax.rsqrt(_dot_split(y * y, gavg) + NORM_EPS) * w


def _gelu_tanh(x):
    return 0.5 * x * (1.0 + jnp.tanh(math.sqrt(2.0 / math.pi) * (x + 0.044715 * (x * x * x))))


def _out_kernel(x_ref, mod_ref, gret_ref, yrf_ref, yrb_ref, u_ref, ys5_ref, ghg_ref, yhf_ref, yhb_ref,
                ggl_ref, ygf_ref, ygb_ref, vec_ref, glu_ref, gavg_ref, wo_ref, nw_ref, *rest, with_router):
    if with_router:
        rw_ref, rb_ref, xo_ref, tok_ref, lg_ref = rest
    else:
        xo_ref, tok_ref = rest
    gavg = gavg_ref[...]
    y_ret = _head_norm(yrf_ref[...] + yrb_ref[...], gavg, vec_ref[0:1, :], True) * _silu(gret_ref[...])
    u = u_ref[...]
    z = _gelu_tanh(ys5_ref[...] + vec_ref[3:4, :] * u)
    y_s5 = z * _sigmoid(_dot(z.astype(BF16), glu_ref[...]) + vec_ref[4:5, :])
    y_hg = _head_norm(yhf_ref[...] + yhb_ref[...], gavg, vec_ref[1:2, :], False) * _silu(ghg_ref[...])
    y_gla = _head_norm(ygf_ref[...] + ygb_ref[...], gavg, vec_ref[2:3, :], False) * _silu(ggl_ref[...])
    acc = _dot(y_ret.astype(BF16), wo_ref[0:256, :])
    acc = acc + _dot(y_s5.astype(BF16), wo_ref[256:512, :])
    acc = acc + _dot(y_hg.astype(BF16), wo_ref[512:768, :])
    acc = acc + _dot(y_gla.astype(BF16), wo_ref[768:1024, :])
    x_new = x_ref[...] + mod_ref[M_GATE1:M_GATE1 + 1, :] * acc
    xo_ref[...] = x_new
    tok = _rms(x_new, nw_ref[...]) * (1.0 + mod_ref[M_SCALE2:M_SCALE2 + 1, :]) + mod_ref[M_SHIFT2:M_SHIFT2 + 1, :]
    tok_ref[...] = tok
    if with_router:
        t_hi = tok.astype(BF16)
        t_lo = (tok - t_hi.astype(F32)).astype(BF16)
        lg_ref[...] = (_dot(t_hi, rw_ref[0]) + _dot(t_lo, rw_ref[0]) + _dot(t_hi, rw_ref[1])) + rb_ref[...]


def _mix_out(xs, mods, p_ret, yrf, yrb, p_s5, ys5, p_hg, yhf, yhb, p_gla, ygf, ygb,
             vec, glu_w, w_out, norm2_w, router, n_ctx_tiles):
    bsz, t_all, d = xs.shape
    nb = math.gcd(SCAN_BATCH, bsz)
    row, mod = _tile_specs(n_ctx_tiles, nb)
    head = np.arange(MIX_W) // (MIX_W // N_HEADS)
    gavg = jnp.asarray((head[:, None] == head[None, :]) / (MIX_W // N_HEADS), BF16)
    ins = [xs, mods, p_ret, yrf, yrb, p_s5, ys5, p_hg, yhf, yhb, p_gla, ygf, ygb,
           vec, glu_w.astype(BF16), gavg, w_out.astype(BF16), norm2_w[None, :]]
    specs = [row(d), mod, row(256, 3), row(256), row(256), row(256), row(256), row(256, 4), row(256), row(256),
             row(256, 2), row(256), row(256), _full(vec.shape), _full(glu_w.shape), _full(gavg.shape),
             _full(w_out.shape), _full((1, d))]
    out_specs = [row(d), row(d)]
    out_shape = [jax.ShapeDtypeStruct((bsz, t_all, d), F32), jax.ShapeDtypeStruct((bsz, t_all, d), F32)]
    if router is not None:
        rw = jnp.pad(router[0], ((0, 0), (0, 128 - N_EXPERTS)))
        rw_hi = rw.astype(BF16)
        rw = jnp.stack([rw_hi, (rw - rw_hi.astype(F32)).astype(BF16)])
        rb = jnp.pad(router[1], (0, 128 - N_EXPERTS))[None, :]
        ins += [rw, rb]
        specs += [_full(rw.shape), _full(rb.shape)]
        out_specs.append(row(128))
        out_shape.append(jax.ShapeDtypeStruct((bsz, t_all, 128), F32))
    batched = (True,) * 13 + (False,) * (len(ins) - 13) + (True,) * len(out_specs)
    return pl.pallas_call(
        _per_batch(functools.partial(_out_kernel, with_router=router is not None), batched),
        grid=(bsz // nb, t_all // ROW_TILE),
        in_specs=specs, out_specs=out_specs, out_shape=out_shape,
        compiler_params=_cparams(("arbitrary", "arbitrary")),
    )(*ins)


FF_SPLIT = 2


def _swiglu(tok16, w1_ref, w3_ref, w2_ref):
    d_ff = w1_ref.shape[-1]
    step = d_ff // FF_SPLIT
    acc = None
    for j in range(FF_SPLIT):
        sl = slice(j * step, (j + 1) * step)
        h = _silu(_dot(tok16, w1_ref[:, sl])) * _dot(tok16, w3_ref[:, sl])
        part = _dot(h.astype(BF16), w2_ref[sl, :])
        acc = part if acc is None else acc + part
    return acc


def _ffn_kernel(x_ref, tok_ref, mod_ref, w1_ref, w3_ref, w2_ref, o_ref):
    f = _swiglu(tok_ref[...].astype(BF16), w1_ref, w3_ref, w2_ref)
    o_ref[...] = x_ref[...] + mod_ref[M_GATE2:M_GATE2 + 1, :] * f


def _dense_ffn(xs, tok, mods, w1, w3, w2, n_ctx_tiles):
    bsz, t_all, d = xs.shape
    row, mod = _tile_specs(n_ctx_tiles)
    once = lambda a: pl.BlockSpec(a.shape, lambda b, t: (0, 0), pipeline_mode=pl.Buffered(1))
    w1, w3, w2 = w1.astype(BF16), w3.astype(BF16), w2.astype(BF16)
    return pl.pallas_call(
        _ffn_kernel, grid=(bsz, t_all // ROW_TILE),
        in_specs=[row(d), row(d), mod, once(w1), once(w3), once(w2)],
        out_specs=row(d), out_shape=jax.ShapeDtypeStruct(xs.shape, F32),
        compiler_params=_cparams(("arbitrary", "arbitrary")),
    )(xs, tok, mods, w1, w3, w2)


MOE_BLOCK = 512


def _routing(logits):
    n = logits.shape[0]
    top_v, top_i = lax.top_k(logits[:, :N_EXPERTS], TOP_K)
    gates = jax.nn.softmax(top_v, axis=-1)
    flat_e = top_i.reshape(-1).astype(jnp.int32)
    onehot = (flat_e[:, None] == jnp.arange(N_EXPERTS, dtype=jnp.int32)[None, :]).astype(jnp.int32)
    csum = jnp.cumsum(onehot, axis=0)
    counts = csum[-1]
    rank = jnp.take_along_axis(csum, flat_e[:, None], axis=1)[:, 0] - 1
    padded = (counts + MOE_BLOCK - 1) // MOE_BLOCK * MOE_BLOCK
    pends = jnp.cumsum(padded)
    dest = (pends - padded)[flat_e] + rank
    n_blocks = -(-(n * TOP_K) // MOE_BLOCK) + N_EXPERTS
    cap = n_blocks * MOE_BLOCK
    flat_tok = jnp.arange(n * TOP_K, dtype=jnp.int32) // TOP_K
    row_tok = jnp.zeros((cap,), jnp.int32).at[dest].set(flat_tok, unique_indices=True)
    block_start = jnp.arange(n_blocks, dtype=jnp.int32) * MOE_BLOCK
    block_e = jnp.minimum(jnp.searchsorted(pends, block_start, side='right'), N_EXPERTS - 1).astype(jnp.int32)
    return dest.reshape(n, TOP_K), gates, row_tok, block_e


def _row_copy(src_hbm, row, dst_ref, sem):
    return pltpu.make_async_copy(src_hbm.at[pl.ds(row, 1), :], dst_ref, sem)


def _rows_wait(src_hbm, dst_ref, sem):
    pltpu.make_async_copy(src_hbm.at[pl.ds(0, dst_ref.shape[0]), :], dst_ref, sem).wait()


def _moe_kernel(be_ref, idx_ref, idx_next_ref, tok_hbm, w1_ref, w3_ref, w2_ref, o_ref, xg_ref, sem):
    del be_ref
    i, j = pl.program_id(0), pl.program_id(1)
    slot = i % 2
    bm = xg_ref.shape[1]
    part_rows = bm // FF_SPLIT

    @pl.when((i == 0) & (j == 0))
    def _():
        def body(r, carry):
            _row_copy(tok_hbm, idx_ref[0, r], xg_ref.at[0, pl.ds(r, 1), :], sem.at[0]).start()
            return carry
        lax.fori_loop(0, bm, body, 0)

    @pl.when(j == 0)
    def _():
        _rows_wait(tok_hbm, xg_ref.at[slot], sem.at[slot])

    for jj in range(FF_SPLIT):
        @pl.when(j == jj)
        def _(jj=jj):
            for r in range(jj * part_rows, (jj + 1) * part_rows):
                _row_copy(tok_hbm, idx_next_ref[0, r], xg_ref.at[1 - slot, pl.ds(r, 1), :], sem.at[1 - slot]).start()

    x16 = xg_ref[slot].astype(BF16)
    h = _silu(_dot(x16, w1_ref[...])) * _dot(x16, w3_ref[...])
    part = _dot(h.astype(BF16), w2_ref[...])

    @pl.when(j == 0)
    def _():
        o_ref[...] = part

    @pl.when(j > 0)
    def _():
        o_ref[...] += part

    @pl.when((i == pl.num_programs(0) - 1) & (j == FF_SPLIT - 1))
    def _():
        _rows_wait(tok_hbm, xg_ref.at[1 - slot], sem.at[1 - slot])


def _moe_experts(tok, row_tok, block_e, w1, w3, w2, layer):
    n, d = tok.shape
    bm = MOE_BLOCK
    n_blocks = block_e.shape[0]
    d_ff = w1.shape[-1]
    step = d_ff // FF_SPLIT
    idx = row_tok.reshape(n_blocks, 1, bm)
    smem_blk = lambda f: pl.BlockSpec((None, 1, bm), f, memory_space=pltpu.SMEM)
    grid_spec = pltpu.PrefetchScalarGridSpec(
        num_scalar_prefetch=1, grid=(n_blocks, FF_SPLIT),
        in_specs=[
            smem_blk(lambda i, j, be: (i, 0, 0)),
            smem_blk(lambda i, j, be: (jnp.minimum(i + 1, n_blocks - 1), 0, 0)),
            pl.BlockSpec(memory_space=pl.ANY),
            pl.BlockSpec((None, None, d, step), lambda i, j, be: (layer, be[i], 0, j)),
            pl.BlockSpec((None, None, d, step), lambda i, j, be: (layer, be[i], 0, j)),
            pl.BlockSpec((None, None, step, d), lambda i, j, be: (layer, be[i], j, 0)),
        ],
        out_specs=pl.BlockSpec((bm, d), lambda i, j, be: (i, 0)),
        scratch_shapes=[pltpu.VMEM((2, bm, d), F32), pltpu.SemaphoreType.DMA((2,))],
    )
    return pl.pallas_call(
        _moe_kernel, grid_spec=grid_spec,
        out_shape=jax.ShapeDtypeStruct((n_blocks * bm, d), F32),
        compiler_params=_cparams(("arbitrary", "arbitrary")),
    )(block_e, idx, idx, tok, w1, w3, w2)


def _combine_kernel(d_ref, dn_ref, x_ref, g_ref, mod_ref, y_hbm, *rest, final):
    fw_ref, (o_ref, buf_ref, sem) = (rest[0], rest[1:]) if final else (None, rest)
    i = pl.program_id(0)
    slot = i % 2
    tm = x_ref.shape[0]

    def gather(idx, to_slot):
        for r in range(tm):
            for k in range(TOP_K):
                _row_copy(y_hbm, idx[k, r], buf_ref.at[to_slot, pl.ds(k * tm + r, 1), :], sem.at[to_slot]).start()

    @pl.when(i == 0)
    def _():
        gather(d_ref, 0)

    @pl.when(i + 1 < pl.num_programs(0))
    def _():
        gather(dn_ref, 1 - slot)

    _rows_wait(y_hbm, buf_ref.at[slot], sem.at[slot])
    f = buf_ref[slot, 0:tm, :] * g_ref[:, 0:1] + buf_ref[slot, tm:2 * tm, :] * g_ref[:, 1:2]
    out = x_ref[...] + mod_ref[M_GATE2:M_GATE2 + 1, :] * f
    o_ref[...] = _rms(out, fw_ref[...]) if final else out


def _moe_combine(xs, mods, ybuf, dest, gates, n_ctx_tiles, final_w=None):
    bsz, t_all, d = xs.shape
    tm = ROW_TILE
    tiles_b = t_all // tm
    n_tiles = bsz * tiles_b
    final = final_w is not None
    skip = n_ctx_tiles if final else 0
    per_b = tiles_b - skip
    n_steps = bsz * per_b
    src = lambda i: (i // per_b) * tiles_b + skip + i % per_b
    dt = dest.reshape(n_tiles, tm, TOP_K).transpose(0, 2, 1)
    smem_blk = lambda f: pl.BlockSpec((None, TOP_K, tm), f, memory_space=pltpu.SMEM)
    ins = [dt, dt, xs.reshape(bsz * t_all, d), gates, mods, ybuf]
    in_specs = [
        smem_blk(lambda i: (src(i), 0, 0)),
        smem_blk(lambda i: (src(jnp.minimum(i + 1, n_steps - 1)), 0, 0)),
        pl.BlockSpec((tm, d), lambda i: (src(i), 0)),
        pl.BlockSpec((tm, TOP_K), lambda i: (src(i), 0)),
        pl.BlockSpec((None, None, 8, d),
                     lambda i: (i // per_b, jnp.where(src(i) % tiles_b >= n_ctx_tiles, 1, 0), 0, 0)),
        pl.BlockSpec(memory_space=pl.ANY),
    ]
    if final:
        ins.append(final_w[None, :])
        in_specs.append(_full((1, d)))
    out = pl.pallas_call(
        functools.partial(_combine_kernel, final=final), grid=(n_steps,),
        in_specs=in_specs,
        out_specs=pl.BlockSpec((tm, d), lambda i: (i, 0)),
        out_shape=jax.ShapeDtypeStruct((n_steps * tm, d), F32),
        scratch_shapes=[pltpu.VMEM((2, TOP_K * tm, d), F32), pltpu.SemaphoreType.DMA((2,))],
        compiler_params=_cparams(("arbitrary",)),
    )(*ins)
    return out.reshape(bsz, per_b * tm, d)


def _final_kernel(x_ref, w_ref, o_ref):
    o_ref[...] = _rms(x_ref[...], w_ref[...])


def _final_norm(xs, w, n_ctx_tiles):
    bsz, t_all, d = xs.shape
    tm = ROW_TILE
    n_lat = t_all // tm - n_ctx_tiles
    return pl.pallas_call(
        _final_kernel, grid=(bsz, n_lat),
        in_specs=[pl.BlockSpec((None, tm, d), lambda b, t: (b, t + n_ctx_tiles, 0)), _full((1, d))],
        out_specs=pl.BlockSpec((None, tm, d), lambda b, t: (b, t, 0)),
        out_shape=jax.ShapeDtypeStruct((bsz, n_lat * tm, d), F32),
        compiler_params=_cparams(("arbitrary", "arbitrary")),
    )(xs, w[None, :])


def kernel(x, c, ctx, c_ctx, ada_w, ada_b, norm1_w, norm2_w, w_in, w_out, ret_gn_w, s5_lam_re, s5_lam_im, s5_log_dt, s5_b_re, s5_b_im, s5_c_re, s5_c_im, s5_d, s5_glu_w, s5_glu_b, hg_lb_logits, hg_norm_w, gla_wa2, gla_ba, gla_norm_w, ffn_w1, ffn_w3, ffn_w2, router_w, router_b, moe_w1, moe_w3, moe_w2, final_norm_w):
    bsz, seq, d = x.shape
    ctx_len = ctx.shape[1]
    depth = ada_w.shape[0]
    assert d == D_MODEL and ctx_len % ROW_TILE == 0 and seq % ROW_TILE == 0
    assert ROW_TILE == SCAN_CHUNK and seq % GRID_W == 0
    n_ctx_tiles = ctx_len // ROW_TILE

    xs = jnp.concatenate([ctx, x], axis=1)
    mods = _modulation(c, c_ctx, ada_w, ada_b)
    cos, sin = _rope_tables(seq, ctx_len)
    lb_sm = jax.nn.softmax(hg_lb_logits.astype(F32), axis=0)
    lower_bounds = jnp.clip(jnp.cumsum(lb_sm, axis=0) - lb_sm[0], 0.0, LB_CEIL)
    w_in_pad = jnp.pad(w_in, ((0, 0), (0, 0), (0, D_IN_PAD - D_IN))).astype(BF16)
    s5_ops = _s5_operators(s5_lam_re, s5_lam_im, s5_log_dt, s5_b_re, s5_b_im, s5_c_re, s5_c_im)
    zeros = jnp.zeros((depth, 3, MIX_W), F32)
    lb_rows = jnp.concatenate([jnp.maximum(lower_bounds, LB_FLOOR)[:, None], (1.0 - lower_bounds)[:, None],
                               zeros, zeros], axis=1)
    vecs = jnp.stack([ret_gn_w, hg_norm_w, gla_norm_w, s5_d, s5_glu_b], axis=1)
    vecs = jnp.concatenate([vecs, zeros], axis=1)
    wa_pad = jnp.stack([jnp.pad(gla_wa2[:, dr], ((0, 0), (GLA_RANK * dr, 128 - GLA_RANK * (dr + 1)), (0, 0)))
                        for dr in range(2)], axis=1)
    w_out16, glu16 = w_out.astype(BF16), s5_glu_w.astype(BF16)
    moe16 = [w.astype(BF16) for w in (moe_w1, moe_w3, moe_w2)]

    for l in range(depth):
        p_ret, p_s5, p_hg, p_gla = _project(xs, mods[l], norm1_w[l], w_in_pad[l], n_ctx_tiles)

        yrf = _ret_scan(p_ret, cos, sin, n_ctx_tiles, False)
        yrb = _ret_scan(p_ret, cos, sin, n_ctx_tiles, True)
        ys5 = _s5_mix(p_s5, *(o[l] for o in s5_ops), ctx_len)
        yhf = _hg_scan(p_hg, lb_rows[l], n_ctx_tiles, False)
        yhb = _hg_scan(p_hg, lb_rows[l], n_ctx_tiles, True)
        ygs = [_gla_scan(p_gla, wa_pad[l, dr], gla_ba[l, dr][None, :], n_ctx_tiles, bool(dr)) for dr in range(2)]

        j = l // 2
        router = (router_w[j], router_b[j]) if l % 2 == 1 else None
        outs = _mix_out(xs, mods[l], p_ret, yrf, yrb, p_s5, ys5, p_hg, yhf, yhb, p_gla, ygs[0], ygs[1],
                        vecs[l], glu16[l], w_out16[l], norm2_w[l], router, n_ctx_tiles)
        if l % 2 == 0:
            x_new, tok = outs
            xs = _dense_ffn(x_new, tok, mods[l], ffn_w1[j], ffn_w3[j], ffn_w2[j], n_ctx_tiles)
        else:
            x_new, tok, logits = outs
            n_tok = bsz * xs.shape[1]
            dest, gates, row_tok, block_e = _routing(logits.reshape(n_tok, 128))
            ybuf = _moe_experts(tok.reshape(n_tok, d), row_tok, block_e, *moe16, j)
            if l == depth - 1:
                return _moe_combine(x_new, mods[l], ybuf, dest, gates, n_ctx_tiles, final_w=final_norm_w)
            xs = _moe_combine(x_new, mods[l], ybuf, dest, gates, n_ctx_tiles)

    return _final_norm(xs, final_norm_w, n_ctx_tiles)
```

```python
import functools
import math

import numpy as np
import jax
import jax.numpy as jnp
from jax import lax
from jax.experimental import pallas as pl
from jax.experimental.pallas import tpu as pltpu

F32 = jnp.float32
BF16 = jnp.bfloat16

D_MODEL = 1024
GRID_W = 64
MIX_W = 256
NORM_EPS = 1e-6
ROPE_BASE = 10000.0
LB_FLOOR = 1e-30
LB_CEIL = 1.0 - 1e-6
N_HEADS = 4
S5_GROUP = 16
S5_GROUPS = 16
S5_STATE = 64
S5_SUB = 16
GLA_RANK = 16
GLA_TAU = 16.0
N_EXPERTS = 8
TOP_K = 2
D_IN = 3360
D_IN_PAD = 3456

SCAN_CHUNK = 256
SCAN_SUB = 16
TAME_LIMIT = 40.0
NEG_BIG = -1e30
VMEM_LIMIT = 56 * 1024 * 1024


def _cparams(sem):
    return pltpu.CompilerParams(dimension_semantics=sem, vmem_limit_bytes=VMEM_LIMIT)


def _full(shape):
    n = len(shape)
    return pl.BlockSpec(shape, lambda *_: (0,) * n)


def _dot(a, b):
    return jnp.dot(a, b, preferred_element_type=F32)


def _dot_nt(a, b):
    return lax.dot_general(a, b, (((1,), (1,)), ((), ())), preferred_element_type=F32)


def _dot_hi(a, b):
    return jnp.dot(a, b, preferred_element_type=F32, precision=lax.Precision.HIGHEST)


def _sigmoid(z):
    return 0.5 * (jnp.tanh(0.5 * z) + 1.0)


def _silu(z):
    return z * _sigmoid(z)


def _log_sigmoid(z):
    return jnp.minimum(z, 0.0) - jnp.log(1.0 + jnp.exp(-jnp.abs(z)))


def _scan_constants(n_heads, wk, wv, reverse):
    c, s, hc = SCAN_CHUNK, SCAN_SUB, SCAN_CHUNK // 2
    i = np.arange(c)[:, None]
    r = np.arange(c)[None, :]
    cum = (r >= i) if reverse else (r <= i)
    ih = np.arange(hc)[:, None]
    jh = np.arange(hc)[None, :]
    causal = (jh >= ih) if reverse else (jh <= ih)
    masks = [((ih // s) == (jh // s)) & causal]
    m = s
    while m < hc:
        same = (ih // (2 * m)) == (jh // (2 * m))
        i_hi, j_hi = (ih % (2 * m)) >= m, (jh % (2 * m)) >= m
        masks.append(same & (~i_hi & j_hi if reverse else i_hi & ~j_hi))
        m *= 2
    dk, dv = wk // n_heads, wv // n_heads
    lk, lv = np.arange(wk), np.arange(wv)
    hmv = (lv[None, :] // dv) == np.arange(n_heads)[:, None]
    bd = (lv[:, None] // dv) == (lk[None, :] // dk)
    return dict(
        cm=jnp.asarray(np.concatenate([cum] * 2, axis=1), BF16),
        pm=jnp.asarray(np.stack([np.tile(x, (1, n_heads)) for x in masks]), F32),
        hmv=jnp.asarray(hmv[:, None, :], BF16),
        bd=jnp.asarray(bd, F32),
        gmat=jnp.asarray(bd.T, BF16),
    )


def _head_scores(qd, k16, n_heads):
    r, wk = qd.shape
    dk = wk // n_heads
    per_col = 128 // dk
    lane_head = lax.broadcasted_iota(jnp.int32, (1, 128), 1) // dk
    blocks = []
    for col in range(wk // 128):
        qc = qd[:, col * 128:(col + 1) * 128]
        lhs = jnp.concatenate([jnp.where(lane_head == h, qc, 0.0) for h in range(per_col)], axis=0)
        sc = _dot_nt(lhs.astype(BF16), k16[:, col * 128:(col + 1) * 128])
        blocks += [sc[h * r:(h + 1) * r, :] for h in range(per_col)]
    return jnp.concatenate(blocks, axis=1)


def _ref_rows(bb_ref, rows, span):
    wk = bb_ref.shape[1]
    pieces = [jnp.zeros((span, wk), F32) if r is None else jnp.broadcast_to(bb_ref[r:r + 1, :], (span, wk))
              for r in rows]
    return pieces[0] if len(pieces) == 1 else jnp.concatenate(pieces, axis=0)


def _scan_core(q, k, v, g, cm_ref, pm_ref, hmv_ref, bd_ref, st_ref, bb_ref, o_ref, *, reverse):
    c, s, hc = SCAN_CHUNK, SCAN_SUB, SCAN_CHUNK // 2
    wk, wv = q.shape[1], v.shape[1]
    n_heads = hmv_ref.shape[0]
    g_hi = g.astype(BF16)
    g_lo = (g - g_hi.astype(F32)).astype(BF16)
    b = _dot(cm_ref[...], jnp.concatenate([g_hi, g_lo], axis=0))
    bb_ref[...] = b
    end = 0 if reverse else c - 1
    b_end = bb_ref[end:end + 1, :]
    st = st_ref[...]
    o_inter = _dot_nt((q * jnp.exp(b)).astype(BF16), st.astype(BF16))
    upd = _dot(v.T.astype(BF16), (k * jnp.exp(b_end - b)).astype(BF16))
    st_ref[...] = st * jnp.exp(b_end) + upd * bd_ref[...]

    v16 = v.astype(BF16)
    vexp = [_expand_heads(v16[h0:h0 + hc], hmv_ref[...]) for h0 in (0, hc)]
    halves = (slice(0, hc), slice(hc, c))

    n_blk = c // s
    if reverse:
        rows = [(i + 1) * s for i in range(n_blk - 1)] + [None]
    else:
        rows = [None] + [i * s - 1 for i in range(1, n_blk)]
    dq = b - _ref_rows(bb_ref, rows, s)
    low = jnp.min(jnp.min(dq, axis=1, keepdims=True), axis=0, keepdims=True)
    qd = q * jnp.exp(dq)
    k16 = (k * jnp.exp(jnp.minimum(-dq, TAME_LIMIT))).astype(BF16)
    pm_diag = pm_ref[0] * (low >= -TAME_LIMIT).astype(F32)
    a = [_head_scores(qd[h], k16[h], n_heads) * pm_diag for h in halves]

    m, lev = s, 1
    while m < hc:
        n_pair = c // (2 * m)
        rows = [p * 2 * m + (m if reverse else m - 1) for p in range(n_pair)]
        ed = jnp.exp(-jnp.abs(b - _ref_rows(bb_ref, rows, 2 * m)))
        qd, k16 = q * ed, (k * ed).astype(BF16)
        a = [a[i] + _head_scores(qd[h], k16[h], n_heads) * pm_ref[lev] for i, h in enumerate(halves)]
        m, lev = 2 * m, lev + 1

    ed = jnp.exp(-jnp.abs(b - _ref_rows(bb_ref, [hc if reverse else hc - 1], c)))
    qd, k16 = q * ed, (k * ed).astype(BF16)
    late, early = (0, 1) if reverse else (1, 0)
    top = _head_scores(qd[halves[late]], k16[halves[early]], n_heads)

    o_ref[halves[early], :] = o_inter[halves[early]] + _dot(a[early].astype(BF16), vexp[early])
    a_late = jnp.concatenate([a[late], top], axis=1).astype(BF16)
    o_ref[halves[late], :] = o_inter[halves[late]] + _dot(a_late, jnp.concatenate([vexp[late], vexp[early]], axis=0))

    return low[0, 0] < -TAME_LIMIT


def _explicit_diagonal(q, k, v, gmat_ref, bb_ref, qb_ref, kb_ref, vb_ref, wj_ref, o_ref, reverse):
    c, s = SCAN_CHUNK, SCAN_SUB
    wv = v.shape[1]
    qb_ref[...] = q
    kb_ref[...] = k
    vb_ref[...] = v
    rows = lax.broadcasted_iota(jnp.int32, (s, 1), 0)

    def blk(ib, carry):
        base = pl.multiple_of(ib * s, s)
        bblk = bb_ref[pl.ds(base, s), :]
        qblk = qb_ref[pl.ds(base, s), :]
        for jo in range(s):
            bj = bb_ref[pl.ds(base + jo, 1), :]
            kj = kb_ref[pl.ds(base + jo, 1), :]
            valid = (rows <= jo) if reverse else (rows >= jo)
            w = qblk * kj * jnp.exp(jnp.where(valid, bblk - bj, NEG_BIG))
            wj_ref[pl.ds(jo * s, s), :] = w.astype(BF16)
        r = _dot(wj_ref[...], gmat_ref[...])
        acc = jnp.zeros((s, wv), F32)
        for jo in range(s):
            acc = acc + r[jo * s:(jo + 1) * s, :] * vb_ref[pl.ds(base + jo, 1), :]
        o_ref[pl.ds(base, s), :] += acc
        return carry

    lax.fori_loop(0, c // s, blk, 0)


def _rope(t, cos, sin_signed):
    outs = []
    for h in range(t.shape[1] // 128):
        th = t[:, h * 128:(h + 1) * 128]
        lane = lax.broadcasted_iota(jnp.int32, th.shape, 1)
        partner = jnp.where((lane % 32) < 16, pltpu.roll(th, 128 - 16, 1), pltpu.roll(th, 16, 1))
        outs.append(th * cos[:, h * 128:(h + 1) * 128] + partner * sin_signed[:, h * 128:(h + 1) * 128])
    return jnp.concatenate(outs, axis=1)


def _expand_heads(x16, hm):
    return (x16[None] * hm).reshape(hm.shape[0] * x16.shape[0], x16.shape[1])


def _ret_constants(reverse):
    c, hc = SCAN_CHUNK, SCAN_CHUNK // 2
    lg = jnp.log1p(-(2.0 ** (-5.0 - jnp.arange(N_HEADS, dtype=F32))))
    lg = lg[::-1] if reverse else lg
    lane = jnp.repeat(lg, MIX_W // N_HEADS)[None, :]
    i = jnp.arange(c, dtype=F32)[:, None]
    steps_in = (c - i) if reverse else (i + 1.0)
    steps_out = i if reverse else (c - 1.0 - i)
    dec = jnp.concatenate([jnp.exp(lane * steps_in), jnp.exp(lane * steps_out),
                           jnp.broadcast_to(jnp.exp(lane * c), (8, MIX_W))], axis=0)
    ih = jnp.arange(hc, dtype=F32)[:, None]
    jh = jnp.arange(hc, dtype=F32)[None, :]
    dist = (jh - ih) if reverse else (ih - jh)
    per_head = lambda d: jnp.concatenate(
        [jnp.where(d >= 0, jnp.exp(lg[h] * jnp.maximum(d, 0.0)), 0.0) for h in range(N_HEADS)], axis=1)
    dm = jnp.stack([per_head(dist), per_head(dist + hc)])
    hd = np.arange(MIX_W) // (MIX_W // N_HEADS)
    hm = jnp.asarray((hd[None, :] == np.arange(N_HEADS)[:, None])[:, None, :], BF16)
    bd = jnp.asarray(hd[:, None] == hd[None, :], F32)
    return dec, dm, hm, bd


SCAN_BATCH = 4


def _ret_kernel(p_ref, cos_ref, sin_ref, dec_ref, dm_ref, hm_ref, bd_ref, o_ref, st_ref, *, reverse):
    @pl.when(pl.program_id(1) == 0)
    def _():
        st_ref[...] = jnp.zeros_like(st_ref)

    c, hc = SCAN_CHUNK, SCAN_CHUNK // 2
    cos, sin = cos_ref[...], sin_ref[...]
    hm = hm_ref[...]
    halves = (slice(0, hc), slice(hc, c))
    late, early = (0, 1) if reverse else (1, 0)
    for bi in range(p_ref.shape[0]):
        q = _rope(p_ref[bi, :, 0:256], cos, sin)
        k = _rope(p_ref[bi, :, 256:512], cos, sin) * (64.0 ** -0.5)
        v = p_ref[bi, :, 512:768]
        st = st_ref[bi]
        o_inter = _dot_nt((q * dec_ref[0:c, :]).astype(BF16), st.astype(BF16))
        upd = _dot(v.T.astype(BF16), (k * dec_ref[c:2 * c, :]).astype(BF16))
        st_ref[bi] = st * dec_ref[2 * c:2 * c + 1, :] + upd * bd_ref[...]

        k16, v16 = k.astype(BF16), v.astype(BF16)
        vx = [_expand_heads(v16[h], hm) for h in halves]
        a_early = _head_scores(q[halves[early]], k16[halves[early]], N_HEADS) * dm_ref[0]
        a_late = _head_scores(q[halves[late]], k16[halves[late]], N_HEADS) * dm_ref[0]
        top = _head_scores(q[halves[late]], k16[halves[early]], N_HEADS) * dm_ref[1]
        o_ref[bi, halves[early], :] = o_inter[halves[early]] + _dot(a_early.astype(BF16), vx[early])
        a_late = jnp.concatenate([a_late, top], axis=1).astype(BF16)
        o_ref[bi, halves[late], :] = (o_inter[halves[late]]
                                      + _dot(a_late, jnp.concatenate([vx[late], vx[early]], axis=0)))


def _hg_inputs(p_ref, extra, reverse):
    (lb_ref,) = extra
    q = p_ref[:, 0:256]
    z = p_ref[:, 512:768] if reverse else p_ref[:, 256:512]
    v = p_ref[:, 768:1024]
    lb_floor, one_m_lb = lb_ref[0:1, :], lb_ref[1:2, :]
    sg = _sigmoid(z)
    return q, one_m_lb * (1.0 - sg), v, jnp.log(lb_floor + one_m_lb * sg)


def _gla_inputs(p_ref, extra, reverse):
    wa_ref, ba_ref = extra
    q = p_ref[:, 0:128]
    k = p_ref[:, 128:256] * (32.0 ** -0.5)
    v = p_ref[:, 256:512]
    low, wa = p_ref[:, 768:896], wa_ref[...]
    l_hi, w_hi = low.astype(BF16), wa.astype(BF16)
    l_lo, w_lo = (low - l_hi.astype(F32)).astype(BF16), (wa - w_hi.astype(F32)).astype(BF16)
    logits = _dot(l_hi, w_hi) + _dot(l_lo, w_hi) + _dot(l_hi, w_lo) + ba_ref[...]
    return q, k, v, _log_sigmoid(logits) * (1.0 / GLA_TAU)


def _scan_kernel(p_ref, *rest, inputs_fn, n_extra, reverse):
    extra, rest = rest[:n_extra], rest[n_extra:]
    (cm_ref, pm_ref, hmv_ref, bd_ref, gmat_ref, o_ref, st_ref, bb_ref, *fallback_scratch) = rest

    @pl.when(pl.program_id(1) == 0)
    def _():
        st_ref[...] = jnp.zeros_like(st_ref)

    n_b = p_ref.shape[0]
    untame = []
    for bi in range(n_b):
        q, k, v, g = inputs_fn(p_ref.at[bi], extra, reverse)
        untame.append(_scan_core(q, k, v, g, cm_ref, pm_ref, hmv_ref, bd_ref,
                                 st_ref.at[bi], bb_ref.at[bi], o_ref.at[bi], reverse=reverse))
    for bi in range(n_b):
        @pl.when(untame[bi])
        def _(bi=bi):
            q, k, v, _ = inputs_fn(p_ref.at[bi], extra, reverse)
            _explicit_diagonal(q, k, v, gmat_ref, bb_ref.at[bi], *fallback_scratch, o_ref.at[bi], reverse)


def _scan_tile_map(n_tiles, n_ctx_tiles, reverse):
    if not reverse:
        return lambda step: step
    return lambda step: jnp.where(step < n_ctx_tiles, n_ctx_tiles - 1 - step, n_tiles - 1 - (step - n_ctx_tiles))


def _scan_call(inputs_fn, p, extra, wk, wv, n_ctx_tiles, reverse):
    bsz, t_all, wp = p.shape
    c, s, nb = SCAN_CHUNK, SCAN_SUB, math.gcd(SCAN_BATCH, bsz)
    n_tiles = t_all // c
    cst = _scan_constants(N_HEADS, wk, wv, reverse)
    tile = _scan_tile_map(n_tiles, n_ctx_tiles, reverse)
    consts = [cst[n] for n in ('cm', 'pm', 'hmv', 'bd', 'gmat')]
    in_specs = [pl.BlockSpec((nb, c, wp), lambda b, t: (b, tile(t), 0))]
    in_specs += [_full(x.shape) for x in extra] + [_full(x.shape) for x in consts]
    return pl.pallas_call(
        functools.partial(_scan_kernel, inputs_fn=inputs_fn, n_extra=len(extra), reverse=reverse),
        grid=(bsz // nb, n_tiles),
        in_specs=in_specs,
        out_specs=pl.BlockSpec((nb, c, wv), lambda b, t: (b, tile(t), 0)),
        out_shape=jax.ShapeDtypeStruct((bsz, t_all, wv), F32),
        scratch_shapes=[
            pltpu.VMEM((nb, wv, wk), F32), pltpu.VMEM((nb, c, wk), F32),
            pltpu.VMEM((c, wk), F32), pltpu.VMEM((c, wk), F32), pltpu.VMEM((c, wv), F32),
            pltpu.VMEM((s * s, wk), BF16),
        ],
        compiler_params=_cparams(("arbitrary", "arbitrary")),
    )(p, *extra, *consts)


def _ret_scan(p_ret, cos, sin, n_ctx_tiles, reverse):
    bsz, t_all, wp = p_ret.shape
    c, nb = SCAN_CHUNK, math.gcd(SCAN_BATCH, bsz)
    tile = _scan_tile_map(t_all // c, n_ctx_tiles, reverse)
    consts = _ret_constants(reverse)
    rows = lambda w: pl.BlockSpec((c, w), lambda b, t: (tile(t), 0))
    return pl.pallas_call(
        functools.partial(_ret_kernel, reverse=reverse),
        grid=(bsz // nb, t_all // c),
        in_specs=[pl.BlockSpec((nb, c, wp), lambda b, t: (b, tile(t), 0)), rows(MIX_W), rows(MIX_W)]
        + [_full(x.shape) for x in consts],
        out_specs=pl.BlockSpec((nb, c, MIX_W), lambda b, t: (b, tile(t), 0)),
        out_shape=jax.ShapeDtypeStruct((bsz, t_all, MIX_W), F32),
        scratch_shapes=[pltpu.VMEM((nb, MIX_W, MIX_W), F32)],
        compiler_params=_cparams(("arbitrary", "arbitrary")),
    )(p_ret, cos, sin, *consts)


def _hg_scan(p_hg, lb_rows, n_ctx_tiles, reverse):
    return _scan_call(_hg_inputs, p_hg, [lb_rows], 256, 256, n_ctx_tiles, reverse)


def _gla_scan(p_gla, wa, ba, n_ctx_tiles, reverse):
    return _scan_call(_gla_inputs, p_gla, [wa, ba], 128, 256, n_ctx_tiles, reverse)


def _rope_tables(seq, ctx_len):
    t_idx = jnp.arange(seq, dtype=jnp.int32)
    row = (t_idx // GRID_W).astype(F32)
    col = (t_idx % GRID_W).astype(F32)
    freqs = ROPE_BASE ** (-jnp.arange(16, dtype=F32) / 16)
    ang_r = row[:, None] * freqs[None, :]
    ang_c = col[:, None] * freqs[None, :]
    cos = jnp.concatenate([jnp.cos(ang_r)] * 2 + [jnp.cos(ang_c)] * 2, axis=1)
    sin = jnp.concatenate([-jnp.sin(ang_r), jnp.sin(ang_r), -jnp.sin(ang_c), jnp.sin(ang_c)], axis=1)
    cos = jnp.concatenate([jnp.ones((ctx_len, 64), F32), cos], axis=0)
    sin = jnp.concatenate([jnp.zeros((ctx_len, 64), F32), sin], axis=0)
    return jnp.tile(cos, (1, N_HEADS)), jnp.tile(sin, (1, N_HEADS))


def _s5_response_kernel(k_ref, rep_ref, o_ref):
    n = S5_SUB
    size = n * S5_GROUP
    diff = (lax.broadcasted_iota(jnp.int32, (size, size), 1) // S5_GROUP
            - lax.broadcasted_iota(jnp.int32, (size, size), 0) // S5_GROUP)

    def tiled(block):
        return jnp.concatenate([_dot(block.astype(BF16), rep_ref[...])] * n, axis=0)

    m = jnp.where(diff == 0, tiled(k_ref[0, 0] + k_ref[1, 0]), 0.0)
    for tau in range(1, n):
        m = jnp.where(diff == tau, tiled(k_ref[0, tau]), m)
        m = jnp.where(diff == -tau, tiled(k_ref[1, tau]), m)
    o_ref[...] = m.astype(o_ref.dtype)


def _s5_response(kern):
    depth, _, g, n, cg, _ = kern.shape
    rep = jnp.asarray(np.tile(np.eye(cg), (1, n)), BF16)
    return pl.pallas_call(
        _s5_response_kernel, grid=(depth, g),
        in_specs=[pl.BlockSpec((None, 2, None, n, cg, cg), lambda l, gi: (l, 0, gi, 0, 0, 0)), _full(rep.shape)],
        out_specs=pl.BlockSpec((None, None, n * cg, n * cg), lambda l, gi: (l, gi, 0, 0)),
        out_shape=jax.ShapeDtypeStruct((depth, g, n * cg, n * cg), BF16),
        compiler_params=_cparams(("arbitrary", "arbitrary")),
    )(kern, rep)


def _s5_operators(lam_re, lam_im, log_dt, b_re, b_im, c_re, c_im):
    n = S5_SUB
    lam = lax.complex(lam_re, lam_im)
    log_a = lam * jnp.exp(log_dt)[..., None]
    a_bar = jnp.exp(log_a)
    b_bar = ((a_bar - 1) / lam)[..., None] * lax.complex(b_re, b_im)
    c_mat = lax.complex(c_re, c_im)
    step = np.arange(n)
    pw = jnp.exp(log_a[..., None, :] * jnp.arange(n + 1, dtype=F32)[:, None])
    kern = jnp.einsum('...cp,...tp,...pd->...tdc', c_mat, pw[..., :n, :], b_bar).real
    m = _s5_response(kern)
    qc =pw[..., n - 1 - step, :][..., None] * b_bar[..., None, :, :]
    qc = jnp.swapaxes(qc, -1, -2)
    z = c_mat[..., None, :, :] * pw[..., step + 1, :][..., None, :]
    z = jnp.moveaxis(z, -1, -3)
    qc = jnp.stack([qc[:, 0], jnp.flip(qc[:, 1], axis=-3)], axis=1)
    z = jnp.stack([z[:, 0], jnp.flip(z[:, 1], axis=-2)], axis=1)
    lead = qc.shape[:3]
    qc =qc.reshape(*lead, n * S5_GROUP, S5_STATE)
    z = z.reshape(*lead, S5_STATE, n * S5_GROUP)
    q = jnp.concatenate([qc.real, qc.imag, qc.imag, qc.real], axis=-1)
    pm = jnp.concatenate([z.real, -z.imag, jnp.zeros_like(z.real), jnp.zeros_like(z.real)], axis=-2)
    a_row = pw[..., n, :]
    a1 = jnp.concatenate([a_row.real] * 4, axis=-1)
    a2 = jnp.concatenate([-a_row.imag, a_row.imag, a_row.imag, -a_row.imag], axis=-1)
    a = jnp.stack([a1[:, 0], a2[:, 0], a1[:, 1], a2[:, 1]], axis=1)
    return (m, jnp.concatenate([q[:, 0], q[:, 1]], axis=-1).astype(BF16),
            pm[:, 0].astype(BF16), pm[:, 1].astype(BF16), a)


S5_GROUP_BLOCK = 8


S5_TILE = S5_SUB * S5_SUB


def _s5_in_kernel(u_ref, q_ref, wf_ref, wb_ref, uf_ref):
    gl = S5_GROUP_BLOCK * S5_GROUP

    def regroup(t, carry):
        x3 = u_ref[pl.ds(pl.multiple_of(t * S5_TILE, S5_TILE), S5_TILE), :].astype(BF16).reshape(S5_SUB, S5_SUB, gl)
        for k in range(S5_GROUP_BLOCK):
            piece = x3[:, :, k * S5_GROUP:(k + 1) * S5_GROUP].reshape(S5_SUB, S5_SUB * S5_GROUP)
            uf_ref[k, pl.ds(pl.multiple_of(t * S5_SUB, S5_SUB), S5_SUB), :] = piece
        return carry

    lax.fori_loop(0, u_ref.shape[0] // S5_TILE, regroup, 0)
    for k in range(S5_GROUP_BLOCK):
        w = _dot(uf_ref[k], q_ref[k])
        wf_ref[:, k, :] = w[:, :256]
        wb_ref[:, k, :] = w[:, 256:]


def _s5_state_kernel(wf_ref, wb_ref, a_ref, hf_ref, hb_ref, sf_ref, sb_ref):
    @pl.when(pl.program_id(0) == 0)
    def _():
        sf_ref[...] = jnp.zeros_like(sf_ref)
        sb_ref[...] = jnp.zeros_like(sb_ref)

    def swap(h):
        return jnp.concatenate([h[:, 128:], h[:, :128]], axis=1)

    n = wf_ref.shape[0]
    hf, hb = sf_ref[...], sb_ref[...]
    a1f, a2f, a1b, a2b = a_ref[0], a_ref[1], a_ref[2], a_ref[3]
    for i in range(n):
        hf_ref[i] = hf
        hf = a1f * hf + a2f * swap(hf) + wf_ref[i]
        j = n - 1 - i
        hb_ref[j] = hb
        hb = a1b * hb + a2b * swap(hb) + wb_ref[j]
    sf_ref[...] = hf
    sb_ref[...] = hb


def _s5_out_kernel(u_ref, m_ref, hf_ref, hb_ref, pf_ref, pb_ref, y_ref, ys_ref, hs_ref):
    for k in range(S5_GROUP_BLOCK):
        hs_ref[k % 2, 0] = hf_ref[:, k, :]
        hs_ref[k % 2, 1] = hb_ref[:, k, :]
        y = _dot(u_ref[k], m_ref[k])
        y = y + _dot(hs_ref[k % 2, 0].astype(BF16), pf_ref[k])
        y = y + _dot(hs_ref[k % 2, 1].astype(BF16), pb_ref[k])
        ys_ref[k] = y

    def regroup(t, carry):
        rows = pl.ds(pl.multiple_of(t * S5_SUB, S5_SUB), S5_SUB)
        pieces = [ys_ref[k, rows, :].reshape(S5_SUB, S5_SUB, S5_GROUP) for k in range(S5_GROUP_BLOCK)]
        y_ref[pl.ds(pl.multiple_of(t * S5_TILE, S5_TILE), S5_TILE), :] = (
            jnp.concatenate(pieces, axis=-1).reshape(S5_TILE, S5_GROUP_BLOCK * S5_GROUP))
        return carry

    lax.fori_loop(0, y_ref.shape[0] // S5_TILE, regroup, 0)


def _s5_mix(u, m, q, pf, pb, a_rows, ctx_len):
    bsz, t_all, _ = u.shape
    n, g = S5_SUB, S5_GROUPS
    n_rows = t_all // n
    blk = 16
    n_steps = n_rows // blk
    n_ctx = (ctx_len // n) // blk
    gb = S5_GROUP_BLOCK
    n_gb = g // gb
    nat_spec = pl.BlockSpec((None, t_all, gb * S5_GROUP), lambda b, gi: (b, 0, gi))
    u_spec = pl.BlockSpec((None, gb, n_rows, 256), lambda b, gi: (b, gi, 0, 0))
    st_spec = pl.BlockSpec((n_rows, gb, 256), lambda b, gi: (0, b * n_gb + gi, 0))
    hshape = jax.ShapeDtypeStruct((n_rows, bsz * g, 256), F32)
    wf, wb, uf = pl.pallas_call(
        _s5_in_kernel, grid=(bsz, n_gb),
        in_specs=[nat_spec, pl.BlockSpec((gb, 256, 512), lambda b, gi: (gi, 0, 0))],
        out_specs=[st_spec, st_spec, u_spec],
        out_shape=[hshape, hshape, jax.ShapeDtypeStruct((bsz, g, n_rows, 256), BF16)],
        compiler_params=_cparams(("arbitrary", "arbitrary")),
    )(u, q)

    a = jnp.tile(a_rows, (1, bsz, 1))
    fwd = lambda s: (s, 0, 0)
    bwd = lambda s: (jnp.where(s < n_ctx, n_ctx - 1 - s, n_steps - 1 - (s - n_ctx)), 0, 0)
    st_block = (blk, bsz * g, 256)
    hf, hb = pl.pallas_call(
        _s5_state_kernel, grid=(n_steps,),
        in_specs=[pl.BlockSpec(st_block, fwd), pl.BlockSpec(st_block, bwd), _full(a.shape)],
        out_specs=[pl.BlockSpec(st_block, fwd), pl.BlockSpec(st_block, bwd)],
        out_shape=[hshape, hshape],
        scratch_shapes=[pltpu.VMEM((bsz * g, 256), F32), pltpu.VMEM((bsz * g, 256), F32)],
        compiler_params=_cparams(("arbitrary",)),
    )(wf, wb, a)

    mat_spec = pl.BlockSpec((gb, 256, 256), lambda b, gi: (gi, 0, 0))
    return pl.pallas_call(
        _s5_out_kernel, grid=(bsz, n_gb),
        in_specs=[u_spec, mat_spec, st_spec, st_spec, mat_spec, mat_spec],
        out_specs=nat_spec,
        out_shape=jax.ShapeDtypeStruct((bsz, t_all, g * S5_GROUP), F32),
        scratch_shapes=[pltpu.VMEM((gb, n_rows, 256), F32), pltpu.VMEM((2, 2, n_rows, 256), F32)],
        compiler_params=_cparams(("arbitrary", "arbitrary")),
    )(uf, m, hf, hb, pf, pb)


ROW_TILE = 256
M_SHIFT1, M_SCALE1, M_GATE1, M_SHIFT2, M_SCALE2, M_GATE2 = range(6)


def _ada_kernel(cond_ref, w_ref, b_ref, o_ref):
    o_ref[...] = _dot_hi(_silu(cond_ref[...]), w_ref[...]) + b_ref[...]


def _modulation(c, c_ctx, ada_w, ada_b):
    depth, d, d6 = ada_w.shape
    bsz = c.shape[0]
    cond = jnp.zeros((8, d), F32).at[:bsz].set(c).at[bsz].set(c_ctx)
    out = pl.pallas_call(
        _ada_kernel, grid=(depth, d6 // d),
        in_specs=[_full((8, d)),
                  pl.BlockSpec((None, d, d), lambda l, j: (l, 0, j)),
                  pl.BlockSpec((None, 1, d), lambda l, j: (l, 0, j))],
        out_specs=pl.BlockSpec((None, 8, d), lambda l, j: (l, 0, j)),
        out_shape=jax.ShapeDtypeStruct((depth, 8, d6), F32),
        compiler_params=_cparams(("arbitrary", "arbitrary")),
    )(cond, ada_w, ada_b[:, None, :])
    out = out.reshape(depth, 8, 6, d)
    lat = out[:, :bsz]
    ctx = jnp.broadcast_to(out[:, bsz][:, None], lat.shape)
    mods = jnp.stack([ctx, lat], axis=2)
    return jnp.pad(mods, ((0, 0), (0, 0), (0, 0), (0, 2), (0, 0)))


def _rms(x, w):
    return x * lax.rsqrt(jnp.mean(x * x, axis=-1, keepdims=True) + NORM_EPS) * w


def _proj_kernel(x_ref, mod_ref, nw_ref, w_ref, ret_ref, s5_ref, hg_ref, gla_ref):
    h = _rms(x_ref[...], nw_ref[...]) * (1.0 + mod_ref[M_SCALE1:M_SCALE1 + 1, :]) + mod_ref[M_SHIFT1:M_SHIFT1 + 1, :]
    p = _dot(h.astype(BF16), w_ref[...])
    ret_ref[...] = p[:, 0:1024]
    s5_ref[...] = p[:, 1024:1280]
    hg_ref[...] = p[:, 1280:2560]
    gla_ref[...] = p[:, 2560:3456]


def _tile_specs(n_ctx_tiles, nb=None):
    tm = ROW_TILE
    row = lambda w, j=0: pl.BlockSpec((nb, tm, w), lambda b, t: (b, t, j))
    mod = pl.BlockSpec((nb, None, 8, D_MODEL), lambda b, t: (b, jnp.where(t >= n_ctx_tiles, 1, 0), 0, 0))
    return row, mod


def _per_batch(fn, batched):
    def wrapped(*refs, **kw):
        n_b = next(r.shape[0] for r, flag in zip(refs, batched) if flag)
        for bi in range(n_b):
            fn(*[r.at[bi] if flag else r for r, flag in zip(refs, batched)], **kw)
    return wrapped


def _project(xs, mods, norm_w, w_pad, n_ctx_tiles):
    bsz, t_all, d = xs.shape
    nb = math.gcd(SCAN_BATCH, bsz)
    row, mod = _tile_specs(n_ctx_tiles, nb)
    widths = (1024, 256, 1280, 896)
    return pl.pallas_call(
        _per_batch(_proj_kernel, (True, True, False, False) + (True,) * len(widths)),
        grid=(bsz // nb, t_all // ROW_TILE),
        in_specs=[row(d), mod, _full((1, d)), _full(w_pad.shape)],
        out_specs=[row(w) for w in widths],
        out_shape=[jax.ShapeDtypeStruct((bsz, t_all, w), F32) for w in widths],
        compiler_params=_cparams(("arbitrary", "arbitrary")),
    )(xs, mods, norm_w[None, :], w_pad)


def _dot_split(x, mat):
    hi = x.astype(BF16)
    lo = (x - hi.astype(F32)).astype(BF16)
    return _dot(hi, mat) + _dot(lo, mat)


def _head_norm(y, gavg, w, center):
    if center:
        y = y - _dot_split(y, gavg)
    return y * lax.rsqrt(_dot_split(y * y, gavg) + NORM_EPS) * w


def _gelu_tanh(x):
    return 0.5 * x * (1.0 + jnp.tanh(math.sqrt(2.0 / math.pi) * (x + 0.044715 * (x * x * x))))


def _out_kernel(x_ref, mod_ref, gret_ref, yrf_ref, yrb_ref, u_ref, ys5_ref, ghg_ref, yhf_ref, yhb_ref,
                ggl_ref, ygf_ref, ygb_ref, vec_ref, glu_ref, gavg_ref, wo_ref, nw_ref, *rest, with_router):
    if with_router:
        rw_ref, rb_ref, xo_ref, tok_ref, lg_ref = rest
    else:
        xo_ref, tok_ref = rest
    gavg = gavg_ref[...]
    y_ret = _head_norm(yrf_ref[...] + yrb_ref[...], gavg, vec_ref[0:1, :], True) * _silu(gret_ref[...])
    u = u_ref[...]
    z = _gelu_tanh(ys5_ref[...] + vec_ref[3:4, :] * u)
    y_s5 = z * _sigmoid(_dot(z.astype(BF16), glu_ref[...]) + vec_ref[4:5, :])
    y_hg = _head_norm(yhf_ref[...] + yhb_ref[...], gavg, vec_ref[1:2, :], False) * _silu(ghg_ref[...])
    y_gla = _head_norm(ygf_ref[...] + ygb_ref[...], gavg, vec_ref[2:3, :], False) * _silu(ggl_ref[...])
    acc = _dot(y_ret.astype(BF16), wo_ref[0:256, :])
    acc = acc + _dot(y_s5.astype(BF16), wo_ref[256:512, :])
    acc = acc + _dot(y_hg.astype(BF16), wo_ref[512:768, :])
    acc = acc + _dot(y_gla.astype(BF16), wo_ref[768:1024, :])
    x_new = x_ref[...] + mod_ref[M_GATE1:M_GATE1 + 1, :] * acc
    xo_ref[...] = x_new
    tok = _rms(x_new, nw_ref[...]) * (1.0 + mod_ref[M_SCALE2:M_SCALE2 + 1, :]) + mod_ref[M_SHIFT2:M_SHIFT2 + 1, :]
    tok_ref[...] = tok
    if with_router:
        t_hi = tok.astype(BF16)
        t_lo = (tok - t_hi.astype(F32)).astype(BF16)
        lg_ref[...] = (_dot(t_hi, rw_ref[0]) + _dot(t_lo, rw_ref[0]) + _dot(t_hi, rw_ref[1])) + rb_ref[...]


def _mix_out(xs, mods, p_ret, yrf, yrb, p_s5, ys5, p_hg, yhf, yhb, p_gla, ygf, ygb,
             vec, glu_w, w_out, norm2_w, router, n_ctx_tiles):
    bsz, t_all, d = xs.shape
    nb = math.gcd(SCAN_BATCH, bsz)
    row, mod = _tile_specs(n_ctx_tiles, nb)
    head = np.arange(MIX_W) // (MIX_W // N_HEADS)
    gavg = jnp.asarray((head[:, None] == head[None, :]) / (MIX_W // N_HEADS), BF16)
    ins = [xs, mods, p_ret, yrf, yrb, p_s5, ys5, p_hg, yhf, yhb, p_gla, ygf, ygb,
           vec, glu_w.astype(BF16), gavg, w_out.astype(BF16), norm2_w[None, :]]
    specs = [row(d), mod, row(256, 3), row(256), row(256), row(256), row(256), row(256, 4), row(256), row(256),
             row(256, 2), row(256), row(256), _full(vec.shape), _full(glu_w.shape), _full(gavg.shape),
             _full(w_out.shape), _full((1, d))]
    out_specs = [row(d), row(d)]
    out_shape = [jax.ShapeDtypeStruct((bsz, t_all, d), F32), jax.ShapeDtypeStruct((bsz, t_all, d), F32)]
    if router is not None:
        rw = jnp.pad(router[0], ((0, 0), (0, 128 - N_EXPERTS)))
        rw_hi = rw.astype(BF16)
        rw = jnp.stack([rw_hi, (rw - rw_hi.astype(F32)).astype(BF16)])
        rb = jnp.pad(router[1], (0, 128 - N_EXPERTS))[None, :]
        ins += [rw, rb]
        specs += [_full(rw.shape), _full(rb.shape)]
        out_specs.append(row(128))
        out_shape.append(jax.ShapeDtypeStruct((bsz, t_all, 128), F32))
    batched = (True,) * 13 + (False,) * (len(ins) - 13) + (True,) * len(out_specs)
    return pl.pallas_call(
        _per_batch(functools.partial(_out_kernel, with_router=router is not None), batched),
        grid=(bsz // nb, t_all // ROW_TILE),
        in_specs=specs, out_specs=out_specs, out_shape=out_shape,
        compiler_params=_cparams(("arbitrary", "arbitrary")),
    )(*ins)


FF_SPLIT = 2


def _swiglu(tok16, w1_ref, w3_ref, w2_ref):
    d_ff = w1_ref.shape[-1]
    step = d_ff // FF_SPLIT
    acc = None
    for j in range(FF_SPLIT):
        sl = slice(j * step, (j + 1) * step)
        h = _silu(_dot(tok16, w1_ref[:, sl])) * _dot(tok16, w3_ref[:, sl])
        part = _dot(h.astype(BF16), w2_ref[sl, :])
        acc = part if acc is None else acc + part
    return acc


def _ffn_kernel(x_ref, tok_ref, mod_ref, w1_ref, w3_ref, w2_ref, o_ref):
    f = _swiglu(tok_ref[...].astype(BF16), w1_ref, w3_ref, w2_ref)
    o_ref[...] = x_ref[...] + mod_ref[M_GATE2:M_GATE2 + 1, :] * f


def _dense_ffn(xs, tok, mods, w1, w3, w2, n_ctx_tiles):
    bsz, t_all, d = xs.shape
    row, mod = _tile_specs(n_ctx_tiles)
    once = lambda a: pl.BlockSpec(a.shape, lambda b, t: (0, 0), pipeline_mode=pl.Buffered(1))
    w1, w3, w2 = w1.astype(BF16), w3.astype(BF16), w2.astype(BF16)
    return pl.pallas_call(
        _ffn_kernel, grid=(bsz, t_all // ROW_TILE),
        in_specs=[row(d), row(d), mod, once(w1), once(w3), once(w2)],
        out_specs=row(d), out_shape=jax.ShapeDtypeStruct(xs.shape, F32),
        compiler_params=_cparams(("arbitrary", "arbitrary")),
    )(xs, tok, mods, w1, w3, w2)


MOE_BLOCK = 512


def _routing(logits):
    n = logits.shape[0]
    top_v, top_i = lax.top_k(logits[:, :N_EXPERTS], TOP_K)
    gates = jax.nn.softmax(top_v, axis=-1)
    flat_e = top_i.reshape(-1).astype(jnp.int32)
    onehot = (flat_e[:, None] == jnp.arange(N_EXPERTS, dtype=jnp.int32)[None, :]).astype(jnp.int32)
    csum = jnp.cumsum(onehot, axis=0)
    counts = csum[-1]
    rank = jnp.take_along_axis(csum, flat_e[:, None], axis=1)[:, 0] - 1
    padded = (counts + MOE_BLOCK - 1) // MOE_BLOCK * MOE_BLOCK
    pends = jnp.cumsum(padded)
    dest = (pends - padded)[flat_e] + rank
    n_blocks = -(-(n * TOP_K) // MOE_BLOCK) + N_EXPERTS
    cap = n_blocks * MOE_BLOCK
    flat_tok = jnp.arange(n * TOP_K, dtype=jnp.int32) // TOP_K
    row_tok = jnp.zeros((cap,), jnp.int32).at[dest].set(flat_tok, unique_indices=True)
    block_start = jnp.arange(n_blocks, dtype=jnp.int32) * MOE_BLOCK
    block_e = jnp.minimum(jnp.searchsorted(pends, block_start, side='right'), N_EXPERTS - 1).astype(jnp.int32)
    return dest.reshape(n, TOP_K), gates, row_tok, block_e


def _row_copy(src_hbm, row, dst_ref, sem):
    return pltpu.make_async_copy(src_hbm.at[pl.ds(row, 1), :], dst_ref, sem)


def _rows_wait(src_hbm, dst_ref, sem):
    pltpu.make_async_copy(src_hbm.at[pl.ds(0, dst_ref.shape[0]), :], dst_ref, sem).wait()


def _moe_kernel(be_ref, idx_ref, idx_next_ref, tok_hbm, w1_ref, w3_ref, w2_ref, o_ref, xg_ref, sem):
    del be_ref
    i, j = pl.program_id(0), pl.program_id(1)
    slot = i % 2
    bm = xg_ref.shape[1]
    part_rows = bm // FF_SPLIT

    @pl.when((i == 0) & (j == 0))
    def _():
        def body(r, carry):
            _row_copy(tok_hbm, idx_ref[0, r], xg_ref.at[0, pl.ds(r, 1), :], sem.at[0]).start()
            return carry
        lax.fori_loop(0, bm, body, 0)

    @pl.when(j == 0)
    def _():
        _rows_wait(tok_hbm, xg_ref.at[slot], sem.at[slot])

    for jj in range(FF_SPLIT):
        @pl.when(j == jj)
        def _(jj=jj):
            for r in range(jj * part_rows, (jj + 1) * part_rows):
                _row_copy(tok_hbm, idx_next_ref[0, r], xg_ref.at[1 - slot, pl.ds(r, 1), :], sem.at[1 - slot]).start()

    x16 = xg_ref[slot].astype(BF16)
    h = _silu(_dot(x16, w1_ref[...])) * _dot(x16, w3_ref[...])
    part = _dot(h.astype(BF16), w2_ref[...])

    @pl.when(j == 0)
    def _():
        o_ref[...] = part

    @pl.when(j > 0)
    def _():
        o_ref[...] += part

    @pl.when((i == pl.num_programs(0) - 1) & (j == FF_SPLIT - 1))
    def _():
        _rows_wait(tok_hbm, xg_ref.at[1 - slot], sem.at[1 - slot])


def _moe_experts(tok, row_tok, block_e, w1, w3, w2, layer):
    n, d = tok.shape
    bm = MOE_BLOCK
    n_blocks = block_e.shape[0]
    d_ff = w1.shape[-1]
    step = d_ff // FF_SPLIT
    idx = row_tok.reshape(n_blocks, 1, bm)
    smem_blk = lambda f: pl.BlockSpec((None, 1, bm), f, memory_space=pltpu.SMEM)
    grid_spec = pltpu.PrefetchScalarGridSpec(
        num_scalar_prefetch=1, grid=(n_blocks, FF_SPLIT),
        in_specs=[
            smem_blk(lambda i, j, be: (i, 0, 0)),
            smem_blk(lambda i, j, be: (jnp.minimum(i + 1, n_blocks - 1), 0, 0)),
            pl.BlockSpec(memory_space=pl.ANY),
            pl.BlockSpec((None, None, d, step), lambda i, j, be: (layer, be[i], 0, j)),
            pl.BlockSpec((None, None, d, step), lambda i, j, be: (layer, be[i], 0, j)),
            pl.BlockSpec((None, None, step, d), lambda i, j, be: (layer, be[i], j, 0)),
        ],
        out_specs=pl.BlockSpec((bm, d), lambda i, j, be: (i, 0)),
        scratch_shapes=[pltpu.VMEM((2, bm, d), F32), pltpu.SemaphoreType.DMA((2,))],
    )
    return pl.pallas_call(
        _moe_kernel, grid_spec=grid_spec,
        out_shape=jax.ShapeDtypeStruct((n_blocks * bm, d), F32),
        compiler_params=_cparams(("arbitrary", "arbitrary")),
    )(block_e, idx, idx, tok, w1, w3, w2)


def _combine_kernel(d_ref, dn_ref, x_ref, g_ref, mod_ref, y_hbm, *rest, final):
    fw_ref, (o_ref, buf_ref, sem) = (rest[0], rest[1:]) if final else (None, rest)
    i = pl.program_id(0)
    slot = i % 2
    tm = x_ref.shape[0]

    def gather(idx, to_slot):
        for r in range(tm):
            for k in range(TOP_K):
                _row_copy(y_hbm, idx[k, r], buf_ref.at[to_slot, pl.ds(k * tm + r, 1), :], sem.at[to_slot]).start()

    @pl.when(i == 0)
    def _():
        gather(d_ref, 0)

    @pl.when(i + 1 < pl.num_programs(0))
    def _():
        gather(dn_ref, 1 - slot)

    _rows_wait(y_hbm, buf_ref.at[slot], sem.at[slot])
    f = buf_ref[slot, 0:tm, :] * g_ref[:, 0:1] + buf_ref[slot, tm:2 * tm, :] * g_ref[:, 1:2]
    out = x_ref[...] + mod_ref[M_GATE2:M_GATE2 + 1, :] * f
    o_ref[...] = _rms(out, fw_ref[...]) if final else out


def _moe_combine(xs, mods, ybuf, dest, gates, n_ctx_tiles, final_w=None):
    bsz, t_all, d = xs.shape
    tm = ROW_TILE
    tiles_b = t_all // tm
    n_tiles = bsz * tiles_b
    final = final_w is not None
    skip = n_ctx_tiles if final else 0
    per_b = tiles_b - skip
    n_steps = bsz * per_b
    src = lambda i: (i // per_b) * tiles_b + skip + i % per_b
    dt = dest.reshape(n_tiles, tm, TOP_K).transpose(0, 2, 1)
    smem_blk = lambda f: pl.BlockSpec((None, TOP_K, tm), f, memory_space=pltpu.SMEM)
    ins = [dt, dt, xs.reshape(bsz * t_all, d), gates, mods, ybuf]
    in_specs = [
        smem_blk(lambda i: (src(i), 0, 0)),
        smem_blk(lambda i: (src(jnp.minimum(i + 1, n_steps - 1)), 0, 0)),
        pl.BlockSpec((tm, d), lambda i: (src(i), 0)),
        pl.BlockSpec((tm, TOP_K), lambda i: (src(i), 0)),
        pl.BlockSpec((None, None, 8, d),
                     lambda i: (i // per_b, jnp.where(src(i) % tiles_b >= n_ctx_tiles, 1, 0), 0, 0)),
        pl.BlockSpec(memory_space=pl.ANY),
    ]
    if final:
        ins.append(final_w[None, :])
        in_specs.append(_full((1, d)))
    out = pl.pallas_call(
        functools.partial(_combine_kernel, final=final), grid=(n_steps,),
        in_specs=in_specs,
        out_specs=pl.BlockSpec((tm, d), lambda i: (i, 0)),
        out_shape=jax.ShapeDtypeStruct((n_steps * tm, d), F32),
        scratch_shapes=[pltpu.VMEM((2, TOP_K * tm, d), F32), pltpu.SemaphoreType.DMA((2,))],
        compiler_params=_cparams(("arbitrary",)),
    )(*ins)
    return out.reshape(bsz, per_b * tm, d)


def _final_kernel(x_ref, w_ref, o_ref):
    o_ref[...] = _rms(x_ref[...], w_ref[...])


def _final_norm(xs, w, n_ctx_tiles):
    bsz, t_all, d = xs.shape
    tm = ROW_TILE
    n_lat = t_all // tm - n_ctx_tiles
    return pl.pallas_call(
        _final_kernel, grid=(bsz, n_lat),
        in_specs=[pl.BlockSpec((None, tm, d), lambda b, t: (b, t + n_ctx_tiles, 0)), _full((1, d))],
        out_specs=pl.BlockSpec((None, tm, d), lambda b, t: (b, t, 0)),
        out_shape=jax.ShapeDtypeStruct((bsz, n_lat * tm, d), F32),
        compiler_params=_cparams(("arbitrary", "arbitrary")),
    )(xs, w[None, :])


def kernel(x, c, ctx, c_ctx, ada_w, ada_b, norm1_w, norm2_w, w_in, w_out, ret_gn_w, s5_lam_re, s5_lam_im, s5_log_dt, s5_b_re, s5_b_im, s5_c_re, s5_c_im, s5_d, s5_glu_w, s5_glu_b, hg_lb_logits, hg_norm_w, gla_wa2, gla_ba, gla_norm_w, ffn_w1, ffn_w3, ffn_w2, router_w, router_b, moe_w1, moe_w3, moe_w2, final_norm_w):
    bsz, seq, d = x.shape
    ctx_len = ctx.shape[1]
    depth = ada_w.shape[0]
    assert d == D_MODEL and ctx_len % ROW_TILE == 0 and seq % ROW_TILE == 0
    assert ROW_TILE == SCAN_CHUNK and seq % GRID_W == 0
    n_ctx_tiles = ctx_len // ROW_TILE

    xs = jnp.concatenate([ctx, x], axis=1)
    mods = _modulation(c, c_ctx, ada_w, ada_b)
    cos, sin = _rope_tables(seq, ctx_len)
    lb_sm = jax.nn.softmax(hg_lb_logits.astype(F32), axis=0)
    lower_bounds = jnp.clip(jnp.cumsum(lb_sm, axis=0) - lb_sm[0], 0.0, LB_CEIL)
    w_in_pad = jnp.pad(w_in, ((0, 0), (0, 0), (0, D_IN_PAD - D_IN))).astype(BF16)
    s5_ops = _s5_operators(s5_lam_re, s5_lam_im, s5_log_dt, s5_b_re, s5_b_im, s5_c_re, s5_c_im)
    zeros = jnp.zeros((depth, 3, MIX_W), F32)
    lb_rows = jnp.concatenate([jnp.maximum(lower_bounds, LB_FLOOR)[:, None], (1.0 - lower_bounds)[:, None],
                               zeros, zeros], axis=1)
    vecs = jnp.stack([ret_gn_w, hg_norm_w, gla_norm_w, s5_d, s5_glu_b], axis=1)
    vecs = jnp.concatenate([vecs, zeros], axis=1)
    wa_pad = jnp.stack([jnp.pad(gla_wa2[:, dr], ((0, 0), (GLA_RANK * dr, 128 - GLA_RANK * (dr + 1)), (0, 0)))
                        for dr in range(2)], axis=1)
    w_out16, glu16 = w_out.astype(BF16), s5_glu_w.astype(BF16)
    moe16 = [w.astype(BF16) for w in (moe_w1, moe_w3, moe_w2)]

    for l in range(depth):
        p_ret, p_s5, p_hg, p_gla = _project(xs, mods[l], norm1_w[l], w_in_pad[l], n_ctx_tiles)

        yrf = _ret_scan(p_ret, cos, sin, n_ctx_tiles, False)
        yrb = _ret_scan(p_ret, cos, sin, n_ctx_tiles, True)
        ys5 = _s5_mix(p_s5, *(o[l] for o in s5_ops), ctx_len)
        yhf = _hg_scan(p_hg, lb_rows[l], n_ctx_tiles, False)
        yhb = _hg_scan(p_hg, lb_rows[l], n_ctx_tiles, True)
        ygs = [_gla_scan(p_gla, wa_pad[l, dr], gla_ba[l, dr][None, :], n_ctx_tiles, bool(dr)) for dr in range(2)]

        j = l // 2
        router = (router_w[j], router_b[j]) if l % 2 == 1 else None
        outs = _mix_out(xs, mods[l], p_ret, yrf, yrb, p_s5, ys5, p_hg, yhf, yhb, p_gla, ygs[0], ygs[1],
                        vecs[l], glu16[l], w_out16[l], norm2_w[l], router, n_ctx_tiles)
        if l % 2 == 0:
            x_new, tok = outs
            xs = _dense_ffn(x_new, tok, mods[l], ffn_w1[j], ffn_w3[j], ffn_w2[j], n_ctx_tiles)
        else:
            x_new, tok, logits = outs
            n_tok = bsz * xs.shape[1]
            dest, gates, row_tok, block_e = _routing(logits.reshape(n_tok, 128))
            ybuf = _moe_experts(tok.reshape(n_tok, d), row_tok, block_e, *moe16, j)
            if l == depth - 1:
                return _moe_combine(x_new, mods[l], ybuf, dest, gates, n_ctx_tiles, final_w=final_norm_w)
            xs = _moe_combine(x_new, mods[l], ybuf, dest, gates, n_ctx_tiles)

    return _final_norm(xs, final_norm_w, n_ctx_tiles)
```
